```python
import math
import numpy as np
import jax, jax.numpy as jnp
from jax import lax

D_MODEL = 2048
BATCH = 1
SEQ = 8192
DEPTH = 2
DEC_BATCH = 32
DEC_SEQ = 4
PAST_LEN = 8192
PAGE_SIZE = 128

RET_H = 4
RET_DK = 128
RET_DV = 256
RET_CHUNK = 128
MOBA_H = 4
MOBA_DH = 128
MOBA_BLOCK = 256
MOBA_TOPK = 3
DIFF_H = 4
DIFF_DH = 64
D_FF = 4 * D_MODEL
ROPE_THETA = 10000.0
Q_BLOCK = 128
NORM_EPS = 1e-6
NEG_INF = -1e30
RET_QK_W = RET_H * RET_DK
RET_V_W = RET_H * RET_DV
MOBA_W = MOBA_H * MOBA_DH
DIFF_W = DIFF_H * 2 * DIFF_DH
IN_W = 2 * RET_QK_W + 2 * RET_V_W + 3 * MOBA_W + 3 * DIFF_W + 3 * D_MODEL

kernel_name = 'hybrid_retention_moba_diffattn_step'


def rms_norm(x, g):
    xf = x.astype(jnp.float32)
    y = xf * lax.rsqrt(jnp.mean(xf * xf, -1, keepdims=True) + NORM_EPS)
    return (y * g.astype(jnp.float32)).astype(x.dtype)


def rope(x, pos):
    d = x.shape[-1]
    half = d // 2
    inv = ROPE_THETA ** (-jnp.arange(half, dtype=jnp.float32) / half)
    ang = pos.astype(jnp.float32)[:, None] * inv[None, :]
    ang = ang.reshape((1, ang.shape[0]) + (1,) * (x.ndim - 3) + (half,))
    cos, sin = jnp.cos(ang), jnp.sin(ang)
    xf = x.astype(jnp.float32)
    x1, x2 = xf[..., :half], xf[..., half:]
    return jnp.concatenate([x1 * cos - x2 * sin, x2 * cos + x1 * sin], -1).astype(x.dtype)


def split_proj(h):
    widths = [RET_QK_W, RET_QK_W, RET_V_W, RET_V_W, MOBA_W, MOBA_W, MOBA_W,
              DIFF_W, DIFF_W, DIFF_W, D_MODEL, D_MODEL, D_MODEL]
    offsets = np.cumsum(widths)[:-1].tolist()
    return jnp.split(h, offsets, axis=-1)


def retention(q, k, v, s0, chunk):
    B, S, H, DK = q.shape
    nc = S // chunk
    log_g = jnp.log(1.0 - 2.0 ** (-5.0 - jnp.arange(H, dtype=jnp.float32)))
    j = jnp.arange(chunk, dtype=jnp.float32)
    diff = j[:, None] - j[None, :]
    dmask = jnp.where((diff >= 0)[None], jnp.exp(jnp.maximum(diff, 0.0)[None] * log_g[:, None, None]), 0.0)
    q_dec = jnp.exp((j + 1.0)[:, None] * log_g[None, :])
    k_dec = jnp.exp((chunk - 1.0 - j)[:, None] * log_g[None, :])
    g_c = jnp.exp(chunk * log_g)

    def to_chunks(t):
        return t.astype(jnp.float32).reshape(B, nc, chunk, H, t.shape[-1]).transpose(1, 0, 2, 3, 4)

    def body(state, qkv):
        qc, kc, vc = qkv
        a = jnp.einsum('bihd,bjhd->bhij', qc, kc) * dmask[None]
        o = jnp.einsum('bhij,bjhe->bihe', a, vc)
        o = o + jnp.einsum('bihd,bhde->bihe', qc, state) * q_dec[None, :, :, None]
        state = state * g_c[None, :, None, None] + jnp.einsum('bjhd,bjhe->bhde', kc * k_dec[None, :, :, None], vc)
        return state, o

    s_fin, o = lax.scan(body, s0.astype(jnp.float32), (to_chunks(q), to_chunks(k), to_chunks(v)))
    o = o.transpose(1, 0, 2, 3, 4).reshape(B, S, H, v.shape[-1])
    return o, s_fin


def moba_select(q, k_mean, n_past, n_sel):
    s = jnp.einsum('bqhd,bnhd->bqhn', q.astype(jnp.float32), k_mean)
    s = jnp.where(jnp.arange(k_mean.shape[1]) < n_past, s, NEG_INF)
    _, idx = lax.top_k(s, n_sel)
    valid = jnp.arange(n_sel) < n_past
    return idx, valid


def moba_attend(q, q_pos, own_k, own_v, own_pos, sel_k, sel_v, sel_valid):
    B, Q, H, D = q.shape
    scale = MOBA_DH ** -0.5
    s_own = jnp.einsum('bqhd,bkhd->bqhk', q, own_k, preferred_element_type=jnp.float32) * scale
    s_own = jnp.where((own_pos[None, :] <= q_pos[:, None])[None, :, None, :], s_own, NEG_INF)
    if sel_k is None:
        p = jax.nn.softmax(s_own, axis=-1)
        out = jnp.einsum('bqhk,bkhd->bqhd', p.astype(own_v.dtype), own_v, preferred_element_type=jnp.float32)
        return out.astype(q.dtype)
    s_sel = jnp.einsum('bqhd,bqhnkd->bqhnk', q, sel_k, preferred_element_type=jnp.float32) * scale
    s_sel = jnp.where(sel_valid[:, None], s_sel, NEG_INF)
    n_own = s_own.shape[-1]
    p = jax.nn.softmax(jnp.concatenate([s_own, s_sel.reshape(B, Q, H, -1)], axis=-1), axis=-1)
    p_own = p[..., :n_own].astype(own_v.dtype)
    p_sel = p[..., n_own:].reshape(s_sel.shape).astype(sel_v.dtype)
    out = (jnp.einsum('bqhk,bkhd->bqhd', p_own, own_v, preferred_element_type=jnp.float32)
           + jnp.einsum('bqhnk,bqhnkd->bqhd', p_sel, sel_v, preferred_element_type=jnp.float32))
    return out.astype(q.dtype)


def moba_prompt(q, k, v):
    B, S, H, D = q.shape
    nb = -(-S // MOBA_BLOCK)
    pad = nb * MOBA_BLOCK - S
    kb = jnp.pad(k, ((0, 0), (0, pad), (0, 0), (0, 0))).reshape(B, nb, MOBA_BLOCK, H, D)
    vb = jnp.pad(v, ((0, 0), (0, pad), (0, 0), (0, 0))).reshape(B, nb, MOBA_BLOCK, H, D)
    k_mean = kb.astype(jnp.float32).mean(axis=2)
    n_sel = min(MOBA_TOPK, nb - 1)
    b_i = jnp.arange(B)[:, None, None, None]
    h_i = jnp.arange(H)[None, None, :, None]
    nq = S // Q_BLOCK
    q_blocks = q.reshape(B, nq, Q_BLOCK, H, D).transpose(1, 0, 2, 3, 4)

    def step(args):
        qi, qb = args
        q_pos = qi * Q_BLOCK + jnp.arange(Q_BLOCK)
        cb = (qi * Q_BLOCK) // MOBA_BLOCK
        own_k = lax.dynamic_index_in_dim(kb, cb, axis=1, keepdims=False)
        own_v = lax.dynamic_index_in_dim(vb, cb, axis=1, keepdims=False)
        own_pos = cb * MOBA_BLOCK + jnp.arange(MOBA_BLOCK)
        if n_sel == 0:
            return moba_attend(qb, q_pos, own_k, own_v, own_pos, None, None, None)
        idx, valid = moba_select(qb, k_mean, cb, n_sel)
        sel_k = kb[b_i, idx, :, h_i, :]
        sel_v = vb[b_i, idx, :, h_i, :]
        return moba_attend(qb, q_pos, own_k, own_v, own_pos, sel_k, sel_v, valid)

    out = lax.map(step, (jnp.arange(nq), q_blocks))
    return out.transpose(1, 0, 2, 3, 4).reshape(B, S, H, D)


def moba_sample(q, k_new, v_new, cache_k, cache_v, page_table):
    B, T, H, D = q.shape
    ppb = MOBA_BLOCK // PAGE_SIZE
    n_full = PAST_LEN // MOBA_BLOCK
    rem = PAST_LEN - n_full * MOBA_BLOCK
    k_pages = cache_k[page_table]
    q_pos = PAST_LEN + jnp.arange(T)
    own_k = jnp.concatenate([k_pages[:, n_full * ppb:].reshape(B, rem, H, D), k_new], axis=1)
    own_v = jnp.concatenate([cache_v[page_table[:, n_full * ppb:]].reshape(B, rem, H, D), v_new], axis=1)
    own_pos = jnp.concatenate([n_full * MOBA_BLOCK + jnp.arange(rem), q_pos])
    n_sel = min(MOBA_TOPK, n_full)
    if n_sel == 0:
        return moba_attend(q, q_pos, own_k, own_v, own_pos, None, None, None)
    k_mean = k_pages[:, :n_full * ppb].astype(jnp.float32).reshape(B, n_full, MOBA_BLOCK, H, D).mean(axis=2)
    idx, valid = moba_select(q, k_mean, n_full, n_sel)
    page_idx = idx[..., None] * ppb + jnp.arange(ppb)
    b5 = jnp.arange(B)[:, None, None, None, None]
    h5 = jnp.arange(H)[None, None, :, None, None]
    phys = page_table[b5, page_idx]
    sel_k = cache_k[phys, :, h5, :].reshape(B, T, H, n_sel, MOBA_BLOCK, D)
    sel_v = cache_v[phys, :, h5, :].reshape(B, T, H, n_sel, MOBA_BLOCK, D)
    return moba_attend(q, q_pos, own_k, own_v, own_pos, sel_k, sel_v, valid)


def diff_lambda(lq1, lk1, lq2, lk2, lam_init):
    f = jnp.float32
    return (jnp.exp(jnp.sum(lq1.astype(f) * lk1.astype(f)))
            - jnp.exp(jnp.sum(lq2.astype(f) * lk2.astype(f))) + lam_init)


def diff_attend(q, k, v, q_pos, k_pos, lam, subln_g, lam_init):
    s = jnp.einsum('bqhcd,bkhcd->bhcqk', q, k, preferred_element_type=jnp.float32) * (DIFF_DH ** -0.5)
    s = jnp.where(k_pos[None, :] <= q_pos[:, None], s, NEG_INF)
    a = jax.nn.softmax(s, axis=-1)
    w = a[:, :, 0] - lam * a[:, :, 1]
    o = jnp.einsum('bhqk,bkhe->bqhe', w.astype(v.dtype), v, preferred_element_type=jnp.float32)
    o = o * lax.rsqrt(jnp.mean(o * o, -1, keepdims=True) + NORM_EPS) * subln_g.astype(jnp.float32)
    return (o * (1.0 - lam_init)).astype(q.dtype)


def diff_prompt(q, k, v, lam, subln_g, lam_init):
    B, S = q.shape[:2]
    nq = S // Q_BLOCK
    k_pos = jnp.arange(S)
    q_blocks = q.reshape((B, nq, Q_BLOCK) + q.shape[2:]).transpose(1, 0, 2, 3, 4, 5)

    def step(args):
        qi, qb = args
        return diff_attend(qb, k, v, qi * Q_BLOCK + jnp.arange(Q_BLOCK), k_pos, lam, subln_g, lam_init)

    out = lax.map(step, (jnp.arange(nq), q_blocks))
    return out.transpose(1, 0, 2, 3, 4).reshape(B, S, DIFF_H, 2 * DIFF_DH)


def diff_sample(q, k_new, v_new, cache_k, cache_v, page_table, lam, subln_g, lam_init):
    B, T = q.shape[:2]
    k_past = cache_k[page_table].reshape((B, PAST_LEN) + cache_k.shape[2:])
    v_past = cache_v[page_table].reshape((B, PAST_LEN) + cache_v.shape[2:])
    k_all = jnp.concatenate([k_past, k_new], axis=1)
    v_all = jnp.concatenate([v_past, v_new], axis=1)
    k_pos = jnp.arange(PAST_LEN + T)
    q_pos = PAST_LEN + jnp.arange(T)
    return diff_attend(q, k_all, v_all, q_pos, k_pos, lam, subln_g, lam_init)


def mixer_inputs(x, pos, norm1_g, w_in):
    B, S, _ = x.shape
    h = rms_norm(x, norm1_g) @ w_in
    rq, rk, rv, rg, mq, mk, mv, dq, dk, dv, gr, gm, gd = split_proj(h)
    rq = rope(rq.reshape(B, S, RET_H, RET_DK), pos)
    rk = rope(rk.reshape(B, S, RET_H, RET_DK), pos) * (RET_DK ** -0.5)
    rv = rv.reshape(B, S, RET_H, RET_DV)
    mq = rope(mq.reshape(B, S, MOBA_H, MOBA_DH), pos)
    mk = rope(mk.reshape(B, S, MOBA_H, MOBA_DH), pos)
    mv = mv.reshape(B, S, MOBA_H, MOBA_DH)
    dq = rope(dq.reshape(B, S, DIFF_H, 2, DIFF_DH), pos)
    dk = rope(dk.reshape(B, S, DIFF_H, 2, DIFF_DH), pos)
    dv = dv.reshape(B, S, DIFF_H, 2 * DIFF_DH)
    return rq, rk, rv, rg, mq, mk, mv, dq, dk, dv, gr, gm, gd


def mixer_merge(x, o_ret, rg, o_moba, o_diff, gr, gm, gd,
                w_up_ret, w_up_moba, w_up_diff, w_out, norm2_g, w_ff1, w_ff2):
    B, S, _ = x.shape
    y_ret = o_ret * lax.rsqrt(jnp.mean(o_ret * o_ret, -1, keepdims=True) + NORM_EPS)
    y_ret = (y_ret.reshape(B, S, RET_V_W) * jax.nn.silu(rg.astype(jnp.float32))).astype(x.dtype)
    b_ret = y_ret @ w_up_ret
    b_moba = o_moba.reshape(B, S, MOBA_W) @ w_up_moba
    b_diff = o_diff.reshape(B, S, DIFF_W) @ w_up_diff
    mix = jax.nn.sigmoid(gr) * b_ret + jax.nn.sigmoid(gm) * b_moba + jax.nn.sigmoid(gd) * b_diff
    x = x + mix @ w_out
    hid = jnp.square(jax.nn.relu(rms_norm(x, norm2_g) @ w_ff1))
    return x + hid @ w_ff2


def setup_inputs(seed: int = 0) -> dict:
    key = jax.random.key(seed)
    ks = jax.random.split(key, 26)
    n_pages = PAST_LEN // PAGE_SIZE
    n_used = DEC_BATCH * n_pages
    n_pool = (5 * n_used + 3) // 4
    f = jnp.float32

    def nrm(k, shape, s=1.0):
        return jax.random.normal(k, shape, f) * s

    page_table = jax.random.permutation(ks[7], n_pool)[:n_used].reshape(DEC_BATCH, n_pages).astype(jnp.int32)
    return {
        'x_prompt': nrm(ks[0], (BATCH, SEQ, D_MODEL)),
        'x_sample': nrm(ks[1], (DEC_BATCH, DEC_SEQ, D_MODEL)),
        'cache_moba_k': nrm(ks[2], (DEPTH, n_pool, PAGE_SIZE, MOBA_H, MOBA_DH)),
        'cache_moba_v': nrm(ks[3], (DEPTH, n_pool, PAGE_SIZE, MOBA_H, MOBA_DH)),
        'cache_diff_k': nrm(ks[4], (DEPTH, n_pool, PAGE_SIZE, DIFF_H, 2, DIFF_DH)),
        'cache_diff_v': nrm(ks[5], (DEPTH, n_pool, PAGE_SIZE, DIFF_H, 2 * DIFF_DH)),
        'state_ret': nrm(ks[6], (DEPTH, DEC_BATCH, RET_H, RET_DK, RET_DV), 0.3),
        'page_table': page_table,
        'norm1_g': 1.0 + nrm(ks[8], (DEPTH, D_MODEL), 0.02),
        'w_in': nrm(ks[9], (DEPTH, D_MODEL, IN_W), D_MODEL ** -0.5),
        'diff_lq1': nrm(ks[10], (DEPTH, DIFF_DH), 0.1),
        'diff_lk1': nrm(ks[11], (DEPTH, DIFF_DH), 0.1),
        'diff_lq2': nrm(ks[12], (DEPTH, DIFF_DH), 0.1),
        'diff_lk2': nrm(ks[13], (DEPTH, DIFF_DH), 0.1),
        'diff_subln_g': 1.0 + nrm(ks[14], (DEPTH, 2 * DIFF_DH), 0.02),
        'w_up_ret': nrm(ks[15], (DEPTH, RET_V_W, D_MODEL), RET_V_W ** -0.5),
        'w_up_moba': nrm(ks[16], (DEPTH, MOBA_W, D_MODEL), MOBA_W ** -0.5),
        'w_up_diff': nrm(ks[17], (DEPTH, DIFF_W, D_MODEL), DIFF_W ** -0.5),
        'w_out': nrm(ks[18], (DEPTH, D_MODEL, D_MODEL), D_MODEL ** -0.5),
        'norm2_g': 1.0 + nrm(ks[19], (DEPTH, D_MODEL), 0.02),
        'w_ff1': nrm(ks[20], (DEPTH, D_MODEL, D_FF), D_MODEL ** -0.5),
        'w_ff2': nrm(ks[21], (DEPTH, D_FF, D_MODEL), D_FF ** -0.5),
        'final_g': 1.0 + nrm(ks[22], (D_MODEL,), 0.02),
    }


def reference(x_prompt, x_sample, cache_moba_k, cache_moba_v, cache_diff_k, cache_diff_v, state_ret,
              page_table, norm1_g, w_in, diff_lq1, diff_lk1, diff_lq2, diff_lk2, diff_subln_g,
              w_up_ret, w_up_moba, w_up_diff, w_out, norm2_g, w_ff1, w_ff2, final_g):
    xp, xs = x_prompt, x_sample
    bp = xp.shape[0]
    pos_p = jnp.arange(xp.shape[1])
    pos_s = PAST_LEN + jnp.arange(xs.shape[1])
    pmk, pmv, pdk, pdv, pst = [], [], [], [], []
    smk, smv, sdk, sdv, sst = [], [], [], [], []
    for l in range(DEPTH):
        lam_init = 0.8 - 0.6 * math.exp(-0.3 * l)
        lam = diff_lambda(diff_lq1[l], diff_lk1[l], diff_lq2[l], diff_lk2[l], lam_init)
        rq, rk, rv, rg, mq, mk, mv, dq, dk, dv, gr, gm, gd = mixer_inputs(xp, pos_p, norm1_g[l], w_in[l])
        o_ret, s_ret = retention(rq, rk, rv, jnp.zeros((bp, RET_H, RET_DK, RET_DV), jnp.float32), RET_CHUNK)
        o_moba = moba_prompt(mq, mk, mv)
        o_diff = diff_prompt(dq, dk, dv, lam, diff_subln_g[l], lam_init)
        xp = mixer_merge(xp, o_ret, rg, o_moba, o_diff, gr, gm, gd, w_up_ret[l], w_up_moba[l], w_up_diff[l],
                         w_out[l], norm2_g[l], w_ff1[l], w_ff2[l])
        pmk.append(mk); pmv.append(mv); pdk.append(dk); pdv.append(dv); pst.append(s_ret.astype(xp.dtype))
        rq, rk, rv, rg, mq, mk, mv, dq, dk, dv, gr, gm, gd = mixer_inputs(xs, pos_s, norm1_g[l], w_in[l])
        o_ret, s_ret = retention(rq, rk, rv, state_ret[l], xs.shape[1])
        o_moba = moba_sample(mq, mk, mv, cache_moba_k[l], cache_moba_v[l], page_table)
        o_diff = diff_sample(dq, dk, dv, cache_diff_k[l], cache_diff_v[l], page_table, lam, diff_subln_g[l], lam_init)
        xs = mixer_merge(xs, o_ret, rg, o_moba, o_diff, gr, gm, gd, w_up_ret[l], w_up_moba[l], w_up_diff[l],
                         w_out[l], norm2_g[l], w_ff1[l], w_ff2[l])
        smk.append(mk); smv.append(mv); sdk.append(dk); sdv.append(dv); sst.append(s_ret.astype(xs.dtype))
    y_prompt = rms_norm(xp, final_g)
    y_sample = rms_norm(xs, final_g)
    p_moba_k = jnp.stack(pmk)
    p_moba_v = jnp.stack(pmv)
    p_diff_k = jnp.stack(pdk)
    p_diff_v = jnp.stack(pdv)
    p_ret_state = jnp.stack(pst)
    s_moba_k = jnp.stack(smk)
    s_moba_v = jnp.stack(smv)
    s_diff_k = jnp.stack(sdk)
    s_diff_v = jnp.stack(sdv)
    s_ret_state = jnp.stack(sst)
    return (y_prompt, y_sample, p_moba_k, p_moba_v, p_diff_k, p_diff_v, p_ret_state,
            s_moba_k, s_moba_v, s_diff_k, s_diff_v, s_ret_state)
```

```python
import functools
import math

import jax
import jax.numpy as jnp
from jax import lax
from jax.experimental import pallas as pl
from jax.experimental.pallas import tpu as pltpu

NORM_EPS = 1e-6
NEG_INF = -1e30
BELOW_NEG_INF = -3e38
ROPE_THETA = 10000.0
MOBA_BLOCK = 256
MOBA_TOPK = 3
LANES = 128
SUBLANES = 8
VMEM_LIMIT_BYTES = 48 * 1024 * 1024

BF16 = jnp.bfloat16
F32 = jnp.float32


def _dot(a, b):
    return jnp.dot(a, b, preferred_element_type=F32)


def _dot_nt(a, b, precision=None):
    return lax.dot_general(a, b, (((1,), (1,)), ((), ())), precision=precision,
                           preferred_element_type=F32)


def _dot_tn(a, b):
    return lax.dot_general(a, b, (((0,), (0,)), ((), ())), preferred_element_type=F32)


def _params(*sem):
    return pltpu.CompilerParams(dimension_semantics=sem, vmem_limit_bytes=VMEM_LIMIT_BYTES)


def _rms(x, g):
    return x * lax.rsqrt(jnp.mean(x * x, axis=-1, keepdims=True) + NORM_EPS) * g


def _is_pow2(v):
    m, _ = math.frexp(v)
    return m == 0.5


EPI_NONE, EPI_ROPE128, EPI_ROPE128_SCALED, EPI_ROPE64 = 0, 1, 2, 3


def _in_proj_kernel(x_ref, g_ref, w_ref, ca_ref, sa_ref, cb_ref, sb1_ref, sb2_ref, o_ref, xn_ref,
                    *, tile_kinds, k_scale):
    j = pl.program_id(1)

    @pl.when(j == 0)
    def _():
        xn_ref[...] = _rms(x_ref[...], g_ref[...]).astype(BF16)

    acc = _dot(xn_ref[...], w_ref[...])
    n_groups = acc.shape[1] // LANES

    def among(kind):
        js = [t for t, kd in enumerate(tile_kinds) if kd == kind]
        cond = j == js[0]
        for t in js[1:]:
            cond = cond | (j == t)
        return cond

    @pl.when(among(EPI_NONE))
    def _():
        o_ref[...] = acc

    def rope128(scale):
        ca, sa = ca_ref[...], sa_ref[...]
        for c in range(n_groups):
            seg = acc[:, c * LANES:(c + 1) * LANES]
            out = seg * ca + pltpu.roll(seg, LANES // 2, axis=1) * sa
            o_ref[:, c * LANES:(c + 1) * LANES] = out if scale is None else out * scale

    @pl.when(among(EPI_ROPE128))
    def _():
        rope128(None)

    @pl.when(among(EPI_ROPE128_SCALED))
    def _():
        rope128(k_scale)

    @pl.when(among(EPI_ROPE64))
    def _():
        cb, sb1, sb2 = cb_ref[...], sb1_ref[...], sb2_ref[...]
        for c in range(n_groups):
            seg = acc[:, c * LANES:(c + 1) * LANES]
            up = pltpu.roll(seg, LANES - LANES // 4, axis=1)
            down = pltpu.roll(seg, LANES // 4, axis=1)
            o_ref[:, c * LANES:(c + 1) * LANES] = seg * cb + up * sb1 + down * sb2


def _rope_tables(pos):
    pos = pos.astype(F32)[:, None]
    half_a, half_b = LANES // 2, LANES // 4
    inv_a = ROPE_THETA ** (-jnp.arange(half_a, dtype=F32) / half_a)
    inv_b = ROPE_THETA ** (-jnp.arange(half_b, dtype=F32) / half_b)
    ang_a = pos * inv_a[None, :]
    ang_b = pos * inv_b[None, :]
    cos_a, sin_a = jnp.cos(ang_a), jnp.sin(ang_a)
    cos_b, sin_b = jnp.cos(ang_b), jnp.sin(ang_b)
    zero_b = jnp.zeros_like(sin_b)
    ca = jnp.concatenate([cos_a, cos_a], -1)
    sa = jnp.concatenate([-sin_a, sin_a], -1)
    cb = jnp.concatenate([cos_b] * 4, -1)
    sb1 = jnp.concatenate([-sin_b, zero_b] * 2, -1)
    sb2 = jnp.concatenate([zero_b, sin_b] * 2, -1)
    return ca, sa, cb, sb1, sb2


def _in_proj(x, g, w_bf16, tables, tile_kinds, k_scale, tm, tn):
    m, d = x.shape
    n = w_bf16.shape[1]
    tab_spec = pl.BlockSpec((tm, LANES), lambda i, j: (i, 0))
    return pl.pallas_call(
        functools.partial(_in_proj_kernel, tile_kinds=tile_kinds, k_scale=k_scale),
        out_shape=jax.ShapeDtypeStruct((m, n), F32),
        grid=(m // tm, n // tn),
        in_specs=[pl.BlockSpec((tm, d), lambda i, j: (i, 0)),
                  pl.BlockSpec((1, d), lambda i, j: (0, 0)),
                  pl.BlockSpec((d, tn), lambda i, j: (0, j)),
                  tab_spec, tab_spec, tab_spec, tab_spec, tab_spec],
        out_specs=pl.BlockSpec((tm, tn), lambda i, j: (i, j)),
        scratch_shapes=[pltpu.VMEM((tm, d), BF16)],
        compiler_params=_params("parallel", "arbitrary"),
        name="in_proj",
    )(x, g.reshape(1, d), w_bf16, *tables)


def _ret_log_g(h):
    return math.log(1.0 - 2.0 ** (-5.0 - h))


def _ret_heads(q_all, k_all, v_all, rg_all, get_state, set_state, write_y, *, n_heads, dk, dv, chunk):
    rows = q_all.shape[0]
    ri = lax.broadcasted_iota(jnp.int32, (rows, rows), 0)
    ci = lax.broadcasted_iota(jnp.int32, (rows, rows), 1)
    diff = (ri - ci).astype(F32)
    pos = lax.broadcasted_iota(jnp.int32, (rows, 1), 0).astype(F32)
    for h in range(n_heads):
        lg = _ret_log_g(h)
        q = q_all[:, h * dk:(h + 1) * dk]
        k = k_all[:, h * dk:(h + 1) * dk]
        vb = v_all[:, h * dv:(h + 1) * dv].astype(BF16)
        qb = q.astype(BF16)
        dmask = jnp.where(diff >= 0, jnp.exp(jnp.maximum(diff, 0.0) * lg), 0.0)
        a = _dot_nt(qb, k.astype(BF16)) * dmask
        state = get_state(h)
        o = _dot(a.astype(BF16), vb) + _dot(qb, state.astype(BF16)) * jnp.exp((pos + 1.0) * lg)
        kd = (k * jnp.exp((chunk - 1.0 - pos) * lg)).astype(BF16)
        set_state(h, state * math.exp(chunk * lg) + _dot_tn(kd, vb))
        yn = o * lax.rsqrt(jnp.mean(o * o, axis=-1, keepdims=True) + NORM_EPS)
        rg = rg_all[:, h * dv:(h + 1) * dv]
        write_y(h, (yn * (rg * jax.nn.sigmoid(rg))).astype(BF16))


def _ret_prompt_kernel(q_ref, k_ref, v_ref, rg_ref, y_ref, s_ref, *, n_heads, dk, dv, chunk):
    @pl.when(pl.program_id(0) == 0)
    def _():
        s_ref[...] = jnp.zeros_like(s_ref)

    def set_state(h, val):
        s_ref[h] = val

    def write_y(h, val):
        y_ref[:, h * dv:(h + 1) * dv] = val

    _ret_heads(q_ref[...], k_ref[...], v_ref[...], rg_ref[...], lambda h: s_ref[h], set_state, write_y,
               n_heads=n_heads, dk=dk, dv=dv, chunk=chunk)


def _ret_prompt(h_all, cols, n_heads, dk, dv, chunk):
    s = h_all.shape[0]
    qk_w, v_w = n_heads * dk, n_heads * dv
    return pl.pallas_call(
        functools.partial(_ret_prompt_kernel, n_heads=n_heads, dk=dk, dv=dv, chunk=chunk),
        out_shape=(jax.ShapeDtypeStruct((s, v_w), BF16),
                   jax.ShapeDtypeStruct((n_heads, dk, dv), F32)),
        grid=(s // chunk,),
        in_specs=[pl.BlockSpec((chunk, qk_w), lambda c: (c, cols["rq"] // qk_w)),
                  pl.BlockSpec((chunk, qk_w), lambda c: (c, cols["rk"] // qk_w)),
                  pl.BlockSpec((chunk, v_w), lambda c: (c, cols["rv"] // v_w)),
                  pl.BlockSpec((chunk, v_w), lambda c: (c, cols["rg"] // v_w))],
        out_specs=(pl.BlockSpec((chunk, v_w), lambda c: (c, 0)),
                   pl.BlockSpec((n_heads, dk, dv), lambda c: (0, 0, 0))),
        compiler_params=_params("arbitrary"),
        name="ret_prompt",
    )(h_all, h_all, h_all, h_all)


def _ret_sample_kernel(q_ref, k_ref, v_ref, rg_ref, s0_ref, y_ref, s_ref, *, n_heads, dk, dv, chunk):
    def set_state(h, val):
        s_ref[0, h] = val

    def write_y(h, val):
        y_ref[0, :, h * dv:(h + 1) * dv] = val

    _ret_heads(q_ref[0], k_ref[0], v_ref[0], rg_ref[0], lambda h: s0_ref[0, h], set_state, write_y,
               n_heads=n_heads, dk=dk, dv=dv, chunk=chunk)


def _ret_sample(q, k, v, rg, state, n_heads, dk, dv, chunk):
    b, rows, _ = q.shape
    qk_w, v_w = n_heads * dk, n_heads * dv

    def row_spec(w):
        return pl.BlockSpec((1, rows, w), lambda i: (i, 0, 0))

    st_spec = pl.BlockSpec((1, n_heads, dk, dv), lambda i: (i, 0, 0, 0))
    return pl.pallas_call(
        functools.partial(_ret_sample_kernel, n_heads=n_heads, dk=dk, dv=dv, chunk=chunk),
        out_shape=(jax.ShapeDtypeStruct((b, rows, v_w), BF16),
                   jax.ShapeDtypeStruct(state.shape, F32)),
        grid=(b,),
        in_specs=[row_spec(qk_w), row_spec(qk_w), row_spec(v_w), row_spec(v_w), st_spec],
        out_specs=(row_spec(v_w), st_spec),
        compiler_params=_params("parallel"),
        name="ret_sample",
    )(q, k, v, rg, state)


def _topk_mask(sc, lane, n_sel):
    sel = jnp.zeros(sc.shape, F32)
    for _ in range(n_sel):
        mx = jnp.max(sc, axis=-1, keepdims=True)
        idx = jnp.min(jnp.where(sc == mx, lane, LANES), axis=-1, keepdims=True)
        pick = lane == idx
        sel = jnp.where(pick & (mx > 0.5 * NEG_INF), 1.0, sel)
        sc = jnp.where(pick, BELOW_NEG_INF, sc)
    return sel


def _moba_prompt_kernel(q_ref, k_ref, v_ref, o_ref, kmean_ref, *, blk, scale, n_sel):
    i = pl.program_id(1)
    s_len, dh = k_ref.shape
    nb = s_len // blk

    @pl.when(i == 0)
    def _():
        kmean_ref[...] = jnp.zeros_like(kmean_ref)
        kmean_ref[0:nb, :] = jnp.mean(k_ref[...].reshape(nb, blk, dh), axis=1)

    q = q_ref[...]
    qb = q.astype(BF16)
    lane = lax.broadcasted_iota(jnp.int32, (blk, LANES), 1)
    sc = _dot_nt(q, kmean_ref[...], precision=lax.Precision.HIGHEST)
    sel = _topk_mask(jnp.where(lane < i, sc, NEG_INF), lane, n_sel)

    own = pl.ds(pl.multiple_of(i * blk, blk), blk)
    s = _dot_nt(qb, k_ref[own, :].astype(BF16)) * scale
    row = lax.broadcasted_iota(jnp.int32, (blk, blk), 0)
    col = lax.broadcasted_iota(jnp.int32, (blk, blk), 1)
    s = jnp.where(col <= row, s, NEG_INF)
    m = jnp.max(s, axis=-1, keepdims=True)
    p = jnp.exp(s - m)
    l = jnp.sum(p, axis=-1, keepdims=True)
    acc = _dot(p.astype(BF16), v_ref[own, :].astype(BF16))

    def body(n, carry):
        m, l, acc = carry
        chosen = jnp.max(jnp.where(lane == n, sel, 0.0), axis=-1, keepdims=True) > 0.5
        past = pl.ds(pl.multiple_of(n * blk, blk), blk)
        s = _dot_nt(qb, k_ref[past, :].astype(BF16)) * scale
        s = jnp.where(chosen, s, NEG_INF)
        m_new = jnp.maximum(m, jnp.max(s, axis=-1, keepdims=True))
        alpha = jnp.exp(m - m_new)
        p = jnp.exp(s - m_new)
        l = alpha * l + jnp.sum(p, axis=-1, keepdims=True)
        acc = alpha * acc + _dot(p.astype(BF16), v_ref[past, :].astype(BF16))
        return m_new, l, acc

    m, l, acc = lax.fori_loop(0, i, body, (m, l, acc))
    o_ref[...] = (acc / l).astype(BF16)


def _moba_prompt(h_all, cols, n_heads, dh):
    s = h_all.shape[0]
    blk = MOBA_BLOCK
    nb = s // blk
    assert s % blk == 0 and nb <= LANES and dh == LANES
    cq, ck, cv = cols["mq"] // dh, cols["mk"] // dh, cols["mv"] // dh
    return pl.pallas_call(
        functools.partial(_moba_prompt_kernel, blk=blk, scale=dh ** -0.5, n_sel=min(MOBA_TOPK, nb - 1)),
        out_shape=jax.ShapeDtypeStruct((s, n_heads * dh), BF16),
        grid=(n_heads, nb),
        in_specs=[pl.BlockSpec((blk, dh), lambda h, i: (i, cq + h)),
                  pl.BlockSpec((s, dh), lambda h, i: (0, ck + h)),
                  pl.BlockSpec((s, dh), lambda h, i: (0, cv + h))],
        out_specs=pl.BlockSpec((blk, dh), lambda h, i: (i, h)),
        scratch_shapes=[pltpu.VMEM((LANES, dh), F32)],
        compiler_params=_params("parallel", "arbitrary"),
        name="moba_prompt",
    )(h_all, h_all, h_all)


def _diag_blocks(x, n_heads, rows_per_head):
    return jnp.concatenate(
        [x[h * rows_per_head:(h + 1) * rows_per_head, h * LANES:(h + 1) * LANES] for h in range(n_heads)], axis=0)


def _new_token_mask(rows, keys, t_pad, n_new):
    r = lax.broadcasted_iota(jnp.int32, (rows, keys), 0) % t_pad
    c = lax.broadcasted_iota(jnp.int32, (rows, keys), 1)
    return (c <= r) & (c < n_new)


def _moba_sample_kernel(pt_ref, q_ref, *refs, ppb, n_heads, t_pad, n_new, scale, n_sel):
    k_refs, v_refs = refs[:ppb], refs[ppb:2 * ppb]
    kn_ref, vn_ref, o_ref, m_ref, l_ref, sc_ref, part_ref = refs[2 * ppb:]
    n = pl.program_id(1)
    nb = pl.num_programs(1)
    rows = q_ref.shape[1]
    qb = q_ref[0]
    lane = lax.broadcasted_iota(jnp.int32, (rows, LANES), 1)

    @pl.when(n == 0)
    def _():
        m_ref[...] = jnp.full_like(m_ref, NEG_INF)
        l_ref[...] = jnp.zeros_like(l_ref)
        sc_ref[...] = jnp.full_like(sc_ref, NEG_INF)

    kb = jnp.concatenate([r[0, 0] for r in k_refs], axis=0).astype(BF16)
    vb = jnp.concatenate([r[0, 0] for r in v_refs], axis=0).astype(BF16)
    s_raw = _dot_nt(qb, kb)
    score = jnp.mean(s_raw, axis=-1, keepdims=True)
    s = s_raw * scale
    m_n = jnp.max(s, axis=-1, keepdims=True)
    p = jnp.exp(s - m_n)
    here = lane == n
    m_ref[...] = jnp.where(here, m_n, m_ref[...])
    l_ref[...] = jnp.where(here, jnp.sum(p, axis=-1, keepdims=True), l_ref[...])
    sc_ref[...] = jnp.where(here, score, sc_ref[...])
    part_ref[n] = _diag_blocks(_dot(p.astype(BF16), vb), n_heads, t_pad)

    @pl.when(n == nb - 1)
    def _():
        sel = _topk_mask(sc_ref[...], lane, n_sel) > 0.5
        s_own = _dot_nt(qb, kn_ref[0].astype(BF16)) * scale
        s_own = jnp.where(_new_token_mask(rows, s_own.shape[1], t_pad, n_new), s_own, NEG_INF)
        m_blocks = m_ref[...]
        m_all = jnp.maximum(jnp.max(jnp.where(sel, m_blocks, NEG_INF), axis=-1, keepdims=True),
                            jnp.max(s_own, axis=-1, keepdims=True))
        p_own = jnp.exp(s_own - m_all)
        w = jnp.where(sel, jnp.exp(m_blocks - m_all), 0.0)
        l_all = jnp.sum(w * l_ref[...], axis=-1, keepdims=True) + jnp.sum(p_own, axis=-1, keepdims=True)
        acc = _diag_blocks(_dot(p_own.astype(BF16), vn_ref[0].astype(BF16)), n_heads, t_pad)
        for b in range(part_ref.shape[0]):
            acc = acc + w[:, b:b + 1] * part_ref[b]
        out = acc / l_all
        for h in range(n_heads):
            o_ref[0, :, h * LANES:(h + 1) * LANES] = out[h * t_pad:(h + 1) * t_pad].astype(BF16)


def _moba_sample(layer, q_exp, k_new, v_new, cache_k, cache_v, page_table, n_heads, dh, n_new):
    b, rows, w = q_exp.shape
    t_pad = rows // n_heads
    page = cache_k.shape[2]
    n_pages = page_table.shape[1]
    ppb = MOBA_BLOCK // page
    nb = n_pages // ppb
    assert ppb * page == MOBA_BLOCK and nb * ppb == n_pages and nb <= LANES and dh == LANES

    def page_spec(g):
        return pl.BlockSpec((1, 1, page, w), lambda i, n, pt: (layer, pt[i, n * ppb + g], 0, 0))

    new_spec = pl.BlockSpec((1, page, w), lambda i, n, pt: (i, 0, 0))
    grid_spec = pltpu.PrefetchScalarGridSpec(
        num_scalar_prefetch=1,
        grid=(b, nb),
        in_specs=[pl.BlockSpec((1, rows, w), lambda i, n, pt: (i, 0, 0))]
                 + [page_spec(g) for g in range(ppb)] * 2 + [new_spec, new_spec],
        out_specs=pl.BlockSpec((1, t_pad, w), lambda i, n, pt: (i, 0, 0)),
        scratch_shapes=[pltpu.VMEM((rows, LANES), F32), pltpu.VMEM((rows, LANES), F32),
                        pltpu.VMEM((rows, LANES), F32), pltpu.VMEM((nb, rows, LANES), F32)],
    )
    return pl.pallas_call(
        functools.partial(_moba_sample_kernel, ppb=ppb, n_heads=n_heads, t_pad=t_pad, n_new=n_new,
                          scale=dh ** -0.5, n_sel=min(MOBA_TOPK, nb)),
        out_shape=jax.ShapeDtypeStruct((b, t_pad, w), BF16),
        grid_spec=grid_spec,
        compiler_params=_params("parallel", "arbitrary"),
        name="moba_sample",
    )(page_table, q_exp, *([cache_k] * ppb), *([cache_v] * ppb), k_new, v_new)


def _diff_lambda(lqk, lam_init):
    s1 = jnp.sum(lqk[0:1] * lqk[1:2], axis=-1, keepdims=True)
    s2 = jnp.sum(lqk[2:3] * lqk[3:4], axis=-1, keepdims=True)
    return jnp.exp(s1) - jnp.exp(s2) + lam_init


def _diff_finish(o1, o2, lqk, g, lam_init):
    od = o1 - _diff_lambda(lqk, lam_init) * o2
    od = od * lax.rsqrt(jnp.mean(od * od, axis=-1, keepdims=True) + NORM_EPS) * g
    return od * (1.0 - lam_init)


def _diff_prompt_kernel(q_ref, k_ref, v_ref, g_ref, lqk_ref, o_ref, m_ref, l_ref, acc_ref,
                        *, tile, scale, lam_init):
    i = pl.program_id(1)
    q = q_ref[...]
    lane = lax.broadcasted_iota(jnp.int32, q.shape, 1)
    first = lane < LANES // 2
    fold = _is_pow2(scale)
    qs = jnp.concatenate([jnp.where(first, q, 0.0), jnp.where(first, 0.0, q)], axis=0)
    qs = (qs * scale if fold else qs).astype(BF16)

    def scores(rows_ds):
        s = _dot_nt(qs, k_ref[rows_ds, :].astype(BF16))
        return s if fold else s * scale

    diag = pl.ds(pl.multiple_of(i * tile, tile), tile)
    s = scores(diag)
    row = lax.broadcasted_iota(jnp.int32, s.shape, 0) % tile
    col = lax.broadcasted_iota(jnp.int32, s.shape, 1)
    s = jnp.where(col <= row, s, NEG_INF)
    m = jnp.max(s, axis=-1, keepdims=True)
    p = jnp.exp(s - m)
    m_ref[...] = m
    l_ref[...] = jnp.sum(p, axis=-1, keepdims=True)
    acc_ref[...] = _dot(p.astype(BF16), v_ref[diag, :].astype(BF16))

    def body(n, carry):
        past = pl.ds(pl.multiple_of(n * tile, tile), tile)
        s = scores(past)
        m_old = m_ref[...]
        m_new = jnp.maximum(m_old, jnp.max(s, axis=-1, keepdims=True))
        alpha = jnp.exp(m_old - m_new)
        p = jnp.exp(s - m_new)
        m_ref[...] = m_new
        l_ref[...] = alpha * l_ref[...] + jnp.sum(p, axis=-1, keepdims=True)
        acc_ref[...] = alpha * acc_ref[...] + _dot(p.astype(BF16), v_ref[past, :].astype(BF16))
        return carry

    lax.fori_loop(0, i, body, 0)
    o = acc_ref[...] / l_ref[...]
    o_ref[...] = _diff_finish(o[:tile], o[tile:], lqk_ref[...], g_ref[...], lam_init).astype(BF16)


def _diff_prompt(h_all, cols, lqk, subln_g, lam_init, n_heads, dh, tile):
    s = h_all.shape[0]
    w = 2 * dh
    assert w == LANES and s % tile == 0
    cq, ck, cv = cols["dq"] // w, cols["dk"] // w, cols["dv"] // w
    return pl.pallas_call(
        functools.partial(_diff_prompt_kernel, tile=tile, scale=dh ** -0.5, lam_init=lam_init),
        out_shape=jax.ShapeDtypeStruct((s, n_heads * w), BF16),
        grid=(n_heads, s // tile),
        in_specs=[pl.BlockSpec((tile, w), lambda h, i: (i, cq + h)),
                  pl.BlockSpec((s, w), lambda h, i: (0, ck + h)),
                  pl.BlockSpec((s, w), lambda h, i: (0, cv + h)),
                  pl.BlockSpec((1, w), lambda h, i: (0, 0)),
                  pl.BlockSpec(lqk.shape, lambda h, i: (0, 0))],
        out_specs=pl.BlockSpec((tile, w), lambda h, i: (i, h)),
        scratch_shapes=[pltpu.VMEM((2 * tile, 1), F32), pltpu.VMEM((2 * tile, 1), F32),
                        pltpu.VMEM((2 * tile, w), F32)],
        compiler_params=_params("parallel", "arbitrary"),
        name="diff_prompt",
    )(h_all, h_all, h_all, subln_g.reshape(1, w), lqk)


def _diff_sample_kernel(pt_ref, q_ref, *refs, pps, n_heads, t_pad, n_new, lam_init):
    k_refs, v_refs = refs[:pps], refs[pps:2 * pps]
    kn_ref, vn_ref, g_ref, lqk_ref, o_ref, m_ref, l_ref, acc_ref = refs[2 * pps:]
    step = pl.program_id(1)
    qb = q_ref[0]
    rows = qb.shape[0]

    def update(s, vb):
        m_old = m_ref[...]
        m_new = jnp.maximum(m_old, jnp.max(s, axis=-1, keepdims=True))
        alpha = jnp.exp(m_old - m_new)
        p = jnp.exp(s - m_new)
        m_ref[...] = m_new
        l_ref[...] = alpha * l_ref[...] + jnp.sum(p, axis=-1, keepdims=True)
        acc_ref[...] = alpha * acc_ref[...] + _dot(p.astype(BF16), vb)

    @pl.when(step == 0)
    def _():
        m_ref[...] = jnp.full_like(m_ref, NEG_INF)
        l_ref[...] = jnp.zeros_like(l_ref)
        acc_ref[...] = jnp.zeros_like(acc_ref)
        s = _dot_nt(qb, kn_ref[0].astype(BF16))
        s = jnp.where(_new_token_mask(rows, s.shape[1], t_pad, n_new), s, NEG_INF)
        update(s, vn_ref[0].astype(BF16))

    for g in range(pps):
        update(_dot_nt(qb, k_refs[g][0, 0].astype(BF16)), v_refs[g][0, 0].astype(BF16))

    @pl.when(step == pl.num_programs(1) - 1)
    def _():
        o = _diag_blocks(acc_ref[...], n_heads, 2 * t_pad) / l_ref[...]
        for h in range(n_heads):
            o1 = o[(2 * h) * t_pad:(2 * h + 1) * t_pad]
            o2 = o[(2 * h + 1) * t_pad:(2 * h + 2) * t_pad]
            o_ref[0, :, h * LANES:(h + 1) * LANES] = _diff_finish(
                o1, o2, lqk_ref[...], g_ref[...], lam_init).astype(BF16)


def _diff_sample(layer, q_exp, k_new, v_new, cache_k, cache_v, page_table, lqk, subln_g, lam_init,
                 n_heads, n_new, pps):
    b, rows, w = q_exp.shape
    t_pad = rows // (2 * n_heads)
    page = cache_k.shape[2]
    n_pages = page_table.shape[1]
    assert n_pages % pps == 0

    def page_spec(g):
        return pl.BlockSpec((1, 1, page, w), lambda i, n, pt: (layer, pt[i, n * pps + g], 0, 0))

    new_spec = pl.BlockSpec((1, page, w), lambda i, n, pt: (i, 0, 0))
    grid_spec = pltpu.PrefetchScalarGridSpec(
        num_scalar_prefetch=1,
        grid=(b, n_pages // pps),
        in_specs=[pl.BlockSpec((1, rows, w), lambda i, n, pt: (i, 0, 0))]
                 + [page_spec(g) for g in range(pps)] * 2 + [new_spec, new_spec,
                    pl.BlockSpec((1, LANES), lambda i, n, pt: (0, 0)),
                    pl.BlockSpec(lqk.shape, lambda i, n, pt: (0, 0))],
        out_specs=pl.BlockSpec((1, t_pad, w), lambda i, n, pt: (i, 0, 0)),
        scratch_shapes=[pltpu.VMEM((rows, 1), F32), pltpu.VMEM((rows, 1), F32), pltpu.VMEM((rows, w), F32)],
    )
    return pl.pallas_call(
        functools.partial(_diff_sample_kernel, pps=pps, n_heads=n_heads, t_pad=t_pad, n_new=n_new,
                          lam_init=lam_init),
        out_shape=jax.ShapeDtypeStruct((b, t_pad, w), BF16),
        grid_spec=grid_spec,
        compiler_params=_params("parallel", "arbitrary"),
        name="diff_sample",
    )(page_table, q_exp, *([cache_k] * pps), *([cache_v] * pps), k_new, v_new, subln_g.reshape(1, LANES), lqk)


def _merge_kernel(x_ref, yr_ref, om_ref, od_ref, gates_ref, wr_ref, wm_ref, wd_ref, wo_ref, o_ref):
    d = x_ref.shape[1]
    mix = (jax.nn.sigmoid(gates_ref[:, 0:d]) * _dot(yr_ref[...], wr_ref[...])
           + jax.nn.sigmoid(gates_ref[:, d:2 * d]) * _dot(om_ref[...], wm_ref[...])
           + jax.nn.sigmoid(gates_ref[:, 2 * d:3 * d]) * _dot(od_ref[...], wd_ref[...]))
    o_ref[...] = x_ref[...] + _dot(mix.astype(BF16), wo_ref[...])


def _merge(x, y_ret, o_moba, o_diff, h_all, gate_col, w_ret, w_moba, w_diff, w_out, tm):
    m, d = x.shape
    assert gate_col % (3 * d) == 0

    def rows(w):
        return pl.BlockSpec((tm, w), lambda i: (i, 0))

    def whole(arr):
        return pl.BlockSpec(arr.shape, lambda i: (0, 0), pipeline_mode=pl.Buffered(1))

    return pl.pallas_call(
        _merge_kernel,
        out_shape=jax.ShapeDtypeStruct((m, d), F32),
        grid=(m // tm,),
        in_specs=[rows(d), rows(y_ret.shape[1]), rows(o_moba.shape[1]), rows(o_diff.shape[1]),
                  pl.BlockSpec((tm, 3 * d), lambda i: (i, gate_col // (3 * d))),
                  whole(w_ret), whole(w_moba), whole(w_diff), whole(w_out)],
        out_specs=rows(d),
        compiler_params=_params("parallel"),
        name="merge_out",
    )(x, y_ret, o_moba, o_diff, h_all, w_ret, w_moba, w_diff, w_out)


def _ffn_kernel(x_ref, g_ref, w1_ref, w2_ref, fg_ref, o_ref, xn_ref, *, final_norm):
    t = pl.program_id(1)

    @pl.when(t == 0)
    def _():
        x = x_ref[...]
        xn_ref[...] = _rms(x, g_ref[...]).astype(BF16)
        o_ref[...] = x

    hid = jnp.square(jnp.maximum(_dot(xn_ref[...], w1_ref[...]), 0.0))
    o_ref[...] += _dot(hid.astype(BF16), w2_ref[...])

    if final_norm:
        @pl.when(t == pl.num_programs(1) - 1)
        def _():
            o_ref[...] = _rms(o_ref[...], fg_ref[...])


def _ffn(x, g, w1, w2, final_g, final_norm, tm, tf):
    m, d = x.shape
    f = w1.shape[1]
    return pl.pallas_call(
        functools.partial(_ffn_kernel, final_norm=final_norm),
        out_shape=jax.ShapeDtypeStruct((m, d), F32),
        grid=(m // tm, f // tf),
        in_specs=[pl.BlockSpec((tm, d), lambda i, t: (i, 0)),
                  pl.BlockSpec((1, d), lambda i, t: (0, 0)),
                  pl.BlockSpec((d, tf), lambda i, t: (0, t)),
                  pl.BlockSpec((tf, d), lambda i, t: (t, 0)),
                  pl.BlockSpec((1, d), lambda i, t: (0, 0))],
        out_specs=pl.BlockSpec((tm, d), lambda i, t: (i, 0)),
        scratch_shapes=[pltpu.VMEM((tm, d), BF16)],
        compiler_params=_params("parallel", "arbitrary"),
        name="ffn",
    )(x, g.reshape(1, d), w1, w2, final_g.reshape(1, d))


def _pad_rows(x, rows):
    return jnp.pad(x, ((0, 0), (0, rows - x.shape[1]), (0, 0)))


def _expand_queries(q, n_groups, t_pad, scale):
    b, t, w = q.shape
    qg = _pad_rows(q * scale, t_pad).reshape(b, t_pad, n_groups, LANES).transpose(0, 2, 1, 3)
    eye = jnp.eye(n_groups, dtype=q.dtype)
    out = qg[:, :, :, None, :] * eye[None, :, None, :, None]
    return out.reshape(b, n_groups * t_pad, w).astype(BF16)


def _expand_diff_queries(q, n_heads, t_pad, scale):
    b, t, w = q.shape
    half = LANES // 2
    qh = _pad_rows(q * scale, t_pad).reshape(b, t_pad, n_heads, 2, half).transpose(0, 2, 3, 1, 4)
    eye = jnp.eye(2 * n_heads, dtype=q.dtype).reshape(n_heads, 2, 2 * n_heads)
    out = qh[:, :, :, :, None, :] * eye[None, :, :, None, :, None]
    return out.reshape(b, n_heads * 2 * t_pad, w).astype(BF16)


def kernel(x_prompt, x_sample, cache_moba_k, cache_moba_v, cache_diff_k, cache_diff_v, state_ret, page_table, norm1_g, w_in, diff_lq1, diff_lk1, diff_lq2, diff_lk2, diff_subln_g, w_up_ret, w_up_moba, w_up_diff, w_out, norm2_g, w_ff1, w_ff2, final_g):
    bp, s_len, d = x_prompt.shape
    bs, t_new, _ = x_sample.shape
    depth, n_pool, page, moba_h, moba_dh = cache_moba_k.shape
    diff_h, diff_dh = cache_diff_k.shape[3], cache_diff_k.shape[5]
    ret_h, ret_dk, ret_dv = state_ret.shape[2:]
    past_len = page_table.shape[1] * page
    assert bp == 1 and ret_dk == LANES and moba_dh == LANES and 2 * diff_dh == LANES
    assert t_new <= SUBLANES and past_len % MOBA_BLOCK == 0

    ret_qk_w, ret_v_w = ret_h * ret_dk, ret_h * ret_dv
    moba_w, diff_w = moba_h * moba_dh, diff_h * 2 * diff_dh
    names = ["rq", "rk", "rv", "rg", "mq", "mk", "mv", "dq", "dk", "dv", "gr", "gm", "gd"]
    widths = [ret_qk_w, ret_qk_w, ret_v_w, ret_v_w, moba_w, moba_w, moba_w, diff_w, diff_w, diff_w, d, d, d]
    cols, off = {}, 0
    for nm, wd in zip(names, widths):
        cols[nm] = off
        off += wd
    in_w = off
    assert w_in.shape == (depth, d, in_w)

    tn = 512
    kinds = {"rq": EPI_ROPE128, "rk": EPI_ROPE128_SCALED, "mq": EPI_ROPE128, "mk": EPI_ROPE128,
             "dq": EPI_ROPE64, "dk": EPI_ROPE64}
    tile_kinds = []
    for nm, wd in zip(names, widths):
        assert wd % tn == 0
        tile_kinds += [kinds.get(nm, EPI_NONE)] * (wd // tn)
    tile_kinds = tuple(tile_kinds)

    m_s = bs * t_new
    xp = x_prompt.reshape(s_len, d)
    xs = x_sample.reshape(m_s, d)
    tables_p = _rope_tables(jnp.arange(s_len))
    tables_s = _rope_tables(jnp.tile(past_len + jnp.arange(t_new), bs))
    tm_p = min(1024, s_len)
    tm_ffn = min(512, s_len)
    tm_merge = min(256, s_len)
    tf = min(512, w_ff1.shape[-1])
    attn_tile = min(256, s_len)
    ret_chunk = min(256, s_len)
    t_pad = SUBLANES

    ck_moba = cache_moba_k.reshape(depth, n_pool, page, moba_w)
    cv_moba = cache_moba_v.reshape(depth, n_pool, page, moba_w)
    ck_diff = cache_diff_k.reshape(depth, n_pool, page, diff_w)
    cv_diff = cache_diff_v.reshape(depth, n_pool, page, diff_w)

    outs = {k: [] for k in ["pmk", "pmv", "pdk", "pdv", "pst", "smk", "smv", "sdk", "sdv", "sst"]}
    for l in range(depth):
        lam_init = 0.8 - 0.6 * math.exp(-0.3 * l)
        lqk = jnp.stack([diff_lq1[l], diff_lk1[l], diff_lq2[l], diff_lk2[l]])
        w_in_l = w_in[l].astype(BF16)
        w_ret_l, w_moba_l, w_diff_l = (w_up_ret[l].astype(BF16), w_up_moba[l].astype(BF16),
                                       w_up_diff[l].astype(BF16))
        w_out_l, w_ff1_l, w_ff2_l = w_out[l].astype(BF16), w_ff1[l].astype(BF16), w_ff2[l].astype(BF16)
        last = l == depth - 1
        k_scale = ret_dk ** -0.5

        def sec(h_all, nm, wd):
            return h_all[:, cols[nm]:cols[nm] + wd]

        hp = _in_proj(xp, norm1_g[l], w_in_l, tables_p, tile_kinds, k_scale, tm_p, tn)
        y_ret, s_fin = _ret_prompt(hp, cols, ret_h, ret_dk, ret_dv, ret_chunk)
        o_moba = _moba_prompt(hp, cols, moba_h, moba_dh)
        o_diff = _diff_prompt(hp, cols, lqk, diff_subln_g[l], lam_init, diff_h, diff_dh, attn_tile)
        x1 = _merge(xp, y_ret, o_moba, o_diff, hp, cols["gr"], w_ret_l, w_moba_l, w_diff_l, w_out_l, tm_merge)
        xp = _ffn(x1, norm2_g[l], w_ff1_l, w_ff2_l, final_g, last, tm_ffn, tf)
        outs["pmk"].append(sec(hp, "mk", moba_w).reshape(bp, s_len, moba_h, moba_dh))
        outs["pmv"].append(sec(hp, "mv", moba_w).reshape(bp, s_len, moba_h, moba_dh))
        outs["pdk"].append(sec(hp, "dk", diff_w).reshape(bp, s_len, diff_h, 2, diff_dh))
        outs["pdv"].append(sec(hp, "dv", diff_w).reshape(bp, s_len, diff_h, 2 * diff_dh))
        outs["pst"].append(s_fin.reshape(bp, ret_h, ret_dk, ret_dv))

        hs = _in_proj(xs, norm1_g[l], w_in_l, tables_s, tile_kinds, k_scale, m_s, tn)

        def sec3(nm, wd):
            return sec(hs, nm, wd).reshape(bs, t_new, wd)

        y_ret_s, s_new = _ret_sample(
            _pad_rows(sec3("rq", ret_qk_w), t_pad), _pad_rows(sec3("rk", ret_qk_w), t_pad),
            _pad_rows(sec3("rv", ret_v_w), t_pad), _pad_rows(sec3("rg", ret_v_w), t_pad),
            state_ret[l], ret_h, ret_dk, ret_dv, t_new)
        mk_s, mv_s = sec3("mk", moba_w), sec3("mv", moba_w)
        o_moba_s = _moba_sample(l, _expand_queries(sec3("mq", moba_w), moba_h, t_pad, 1.0),
                                _pad_rows(mk_s, page), _pad_rows(mv_s, page), ck_moba, cv_moba, page_table,
                                moba_h, moba_dh, t_new)
        dk_s, dv_s = sec3("dk", diff_w), sec3("dv", diff_w)
        diff_scale = diff_dh ** -0.5
        assert _is_pow2(diff_scale)
        o_diff_s = _diff_sample(l, _expand_diff_queries(sec3("dq", diff_w), diff_h, t_pad, diff_scale),
                                _pad_rows(dk_s, page), _pad_rows(dv_s, page), ck_diff, cv_diff, page_table,
                                lqk, diff_subln_g[l], lam_init, diff_h, t_new, min(4, page_table.shape[1]))

        def unpad(o):
            return o[:, :t_new].reshape(m_s, o.shape[-1])

        x1s = _merge(xs, unpad(y_ret_s), unpad(o_moba_s), unpad(o_diff_s), hs, cols["gr"],
                     w_ret_l, w_moba_l, w_diff_l, w_out_l, m_s)
        xs = _ffn(x1s, norm2_g[l], w_ff1_l, w_ff2_l, final_g, last, m_s, tf)
        outs["smk"].append(mk_s.reshape(bs, t_new, moba_h, moba_dh))
        outs["smv"].append(mv_s.reshape(bs, t_new, moba_h, moba_dh))
        outs["sdk"].append(dk_s.reshape(bs, t_new, diff_h, 2, diff_dh))
        outs["sdv"].append(dv_s.reshape(bs, t_new, diff_h, 2 * diff_dh))
        outs["sst"].append(s_new)

    st = {k: jnp.stack(v) for k, v in outs.items()}
    return (xp.reshape(bp, s_len, d), xs.reshape(bs, t_new, d),
            st["pmk"], st["pmv"], st["pdk"], st["pdv"], st["pst"],
            st["smk"], st["smv"], st["sdk"], st["sdv"], st["sst"])
```

```python
import functools
import math

import jax
import jax.numpy as jnp
from jax import lax
from jax.experimental import pallas as pl
from jax.experimental.pallas import tpu as pltpu

NORM_EPS = 1e-6
NEG_INF = -1e30
BELOW_NEG_INF = -3e38
ROPE_THETA = 10000.0
MOBA_BLOCK = 256
MOBA_TOPK = 3
LANES = 128
SUBLANES = 8
VMEM_LIMIT_BYTES = 48 * 1024 * 1024
LOG2_E = 1.4426950408889634

BF16 = jnp.bfloat16
F32 = jnp.float32


def _dot(a, b):
    return jnp.dot(a, b, preferred_element_type=F32)


def _dot_nt(a, b, precision=None):
    return lax.dot_general(a, b, (((1,), (1,)), ((), ())), precision=precision,
                           preferred_element_type=F32)


def _dot_tn(a, b):
    return lax.dot_general(a, b, (((0,), (0,)), ((), ())), preferred_element_type=F32)


def _params(*sem):
    return pltpu.CompilerParams(dimension_semantics=sem, vmem_limit_bytes=VMEM_LIMIT_BYTES)


def _rms(x, g):
    return x * lax.rsqrt(jnp.mean(x * x, axis=-1, keepdims=True) + NORM_EPS) * g


def _is_pow2(v):
    m, _ = math.frexp(v)
    return m == 0.5


EPI_NONE, EPI_ROPE128, EPI_ROPE128_SCALED, EPI_ROPE64 = 0, 1, 2, 3


def _in_proj_kernel(x_ref, g_ref, w_ref, ca_ref, sa_ref, cb_ref, sb1_ref, sb2_ref, o_ref, xn_ref,
                    *, tile_kinds, k_scale):
    j = pl.program_id(1)

    @pl.when(j == 0)
    def _():
        xn_ref[...] = _rms(x_ref[...], g_ref[...]).astype(BF16)

    acc = _dot(xn_ref[...], w_ref[...])
    n_groups = acc.shape[1] // LANES

    def among(kind):
        js = [t for t, kd in enumerate(tile_kinds) if kd == kind]
        cond = j == js[0]
        for t in js[1:]:
            cond = cond | (j == t)
        return cond

    @pl.when(among(EPI_NONE))
    def _():
        o_ref[...] = acc

    def rope128(scale):
        ca, sa = ca_ref[...], sa_ref[...]
        for c in range(n_groups):
            seg = acc[:, c * LANES:(c + 1) * LANES]
            out = seg * ca + pltpu.roll(seg, LANES // 2, axis=1) * sa
            o_ref[:, c * LANES:(c + 1) * LANES] = out if scale is None else out * scale

    @pl.when(among(EPI_ROPE128))
    def _():
        rope128(None)

    @pl.when(among(EPI_ROPE128_SCALED))
    def _():
        rope128(k_scale)

    @pl.when(among(EPI_ROPE64))
    def _():
        cb, sb1, sb2 = cb_ref[...], sb1_ref[...], sb2_ref[...]
        for c in range(n_groups):
            seg = acc[:, c * LANES:(c + 1) * LANES]
            up = pltpu.roll(seg, LANES - LANES // 4, axis=1)
            down = pltpu.roll(seg, LANES // 4, axis=1)
            o_ref[:, c * LANES:(c + 1) * LANES] = seg * cb + up * sb1 + down * sb2


def _rope_tables(pos):
    pos = pos.astype(F32)[:, None]
    half_a, half_b = LANES // 2, LANES // 4
    inv_a = ROPE_THETA ** (-jnp.arange(half_a, dtype=F32) / half_a)
    inv_b = ROPE_THETA ** (-jnp.arange(half_b, dtype=F32) / half_b)
    ang_a = pos * inv_a[None, :]
    ang_b = pos * inv_b[None, :]
    cos_a, sin_a = jnp.cos(ang_a), jnp.sin(ang_a)
    cos_b, sin_b = jnp.cos(ang_b), jnp.sin(ang_b)
    zero_b = jnp.zeros_like(sin_b)
    ca = jnp.concatenate([cos_a, cos_a], -1)
    sa = jnp.concatenate([-sin_a, sin_a], -1)
    cb = jnp.concatenate([cos_b] * 4, -1)
    sb1 = jnp.concatenate([-sin_b, zero_b] * 2, -1)
    sb2 = jnp.concatenate([zero_b, sin_b] * 2, -1)
    return ca, sa, cb, sb1, sb2


def _in_proj(x, g, w_bf16, tables, tile_kinds, k_scale, tm, tn):
    m, d = x.shape
    n = w_bf16.shape[1]
    tab_spec = pl.BlockSpec((tm, LANES), lambda i, j: (i, 0))
    return pl.pallas_call(
        functools.partial(_in_proj_kernel, tile_kinds=tile_kinds, k_scale=k_scale),
        out_shape=jax.ShapeDtypeStruct((m, n), F32),
        grid=(m // tm, n // tn),
        in_specs=[pl.BlockSpec((tm, d), lambda i, j: (i, 0)),
                  pl.BlockSpec((1, d), lambda i, j: (0, 0)),
                  pl.BlockSpec((d, tn), lambda i, j: (0, j)),
                  tab_spec, tab_spec, tab_spec, tab_spec, tab_spec],
        out_specs=pl.BlockSpec((tm, tn), lambda i, j: (i, j)),
        scratch_shapes=[pltpu.VMEM((tm, d), BF16)],
        compiler_params=_params("parallel", "arbitrary"),
        name="in_proj",
    )(x, g.reshape(1, d), w_bf16, *tables)


def _ret_log_g(h):
    return math.log(1.0 - 2.0 ** (-5.0 - h))


def _ret_heads(q_all, k_all, v_all, rg_all, get_state, set_state, write_y, *, n_heads, dk, dv, chunk):
    rows = q_all.shape[0]
    ri = lax.broadcasted_iota(jnp.int32, (rows, rows), 0)
    ci = lax.broadcasted_iota(jnp.int32, (rows, rows), 1)
    diff = (ri - ci).astype(F32)
    pos = lax.broadcasted_iota(jnp.int32, (rows, 1), 0).astype(F32)
    for h in range(n_heads):
        lg = _ret_log_g(h)
        q = q_all[:, h * dk:(h + 1) * dk]
        k = k_all[:, h * dk:(h + 1) * dk]
        vb = v_all[:, h * dv:(h + 1) * dv].astype(BF16)
        qb = q.astype(BF16)
        dmask = jnp.where(diff >= 0, jnp.exp(jnp.maximum(diff, 0.0) * lg), 0.0)
        a = _dot_nt(qb, k.astype(BF16)) * dmask
        state = get_state(h)
        o = _dot(a.astype(BF16), vb) + _dot(qb, state.astype(BF16)) * jnp.exp((pos + 1.0) * lg)
        kd = (k * jnp.exp((chunk - 1.0 - pos) * lg)).astype(BF16)
        set_state(h, state * math.exp(chunk * lg) + _dot_tn(kd, vb))
        yn = o * lax.rsqrt(jnp.mean(o * o, axis=-1, keepdims=True) + NORM_EPS)
        rg = rg_all[:, h * dv:(h + 1) * dv]
        write_y(h, (yn * (rg * jax.nn.sigmoid(rg))).astype(BF16))


def _ret_prompt_kernel(q_ref, k_ref, v_ref, rg_ref, y_ref, s_ref, *, n_heads, dk, dv, chunk):
    @pl.when(pl.program_id(0) == 0)
    def _():
        s_ref[...] = jnp.zeros_like(s_ref)

    def set_state(h, val):
        s_ref[h] = val

    def write_y(h, val):
        y_ref[:, h * dv:(h + 1) * dv] = val

    _ret_heads(q_ref[...], k_ref[...], v_ref[...], rg_ref[...], lambda h: s_ref[h], set_state, write_y,
               n_heads=n_heads, dk=dk, dv=dv, chunk=chunk)


def _ret_prompt(h_all, cols, n_heads, dk, dv, chunk):
    s = h_all.shape[0]
    qk_w, v_w = n_heads * dk, n_heads * dv
    return pl.pallas_call(
        functools.partial(_ret_prompt_kernel, n_heads=n_heads, dk=dk, dv=dv, chunk=chunk),
        out_shape=(jax.ShapeDtypeStruct((s, v_w), BF16),
                   jax.ShapeDtypeStruct((n_heads, dk, dv), F32)),
        grid=(s // chunk,),
        in_specs=[pl.BlockSpec((chunk, qk_w), lambda c: (c, cols["rq"] // qk_w)),
                  pl.BlockSpec((chunk, qk_w), lambda c: (c, cols["rk"] // qk_w)),
                  pl.BlockSpec((chunk, v_w), lambda c: (c, cols["rv"] // v_w)),
                  pl.BlockSpec((chunk, v_w), lambda c: (c, cols["rg"] // v_w))],
        out_specs=(pl.BlockSpec((chunk, v_w), lambda c: (c, 0)),
                   pl.BlockSpec((n_heads, dk, dv), lambda c: (0, 0, 0))),
        compiler_params=_params("arbitrary"),
        name="ret_prompt",
    )(h_all, h_all, h_all, h_all)


def _ret_sample_kernel(q_ref, k_ref, v_ref, rg_ref, s0_ref, y_ref, s_ref, *, n_heads, dk, dv, chunk):
    def set_state(h, val):
        s_ref[0, h] = val

    def write_y(h, val):
        y_ref[0, :, h * dv:(h + 1) * dv] = val

    _ret_heads(q_ref[0], k_ref[0], v_ref[0], rg_ref[0], lambda h: s0_ref[0, h], set_state, write_y,
               n_heads=n_heads, dk=dk, dv=dv, chunk=chunk)


def _ret_sample(q, k, v, rg, state, n_heads, dk, dv, chunk):
    b, rows, _ = q.shape
    qk_w, v_w = n_heads * dk, n_heads * dv

    def row_spec(w):
        return pl.BlockSpec((1, rows, w), lambda i: (i, 0, 0))

    st_spec = pl.BlockSpec((1, n_heads, dk, dv), lambda i: (i, 0, 0, 0))
    return pl.pallas_call(
        functools.partial(_ret_sample_kernel, n_heads=n_heads, dk=dk, dv=dv, chunk=chunk),
        out_shape=(jax.ShapeDtypeStruct((b, rows, v_w), BF16),
                   jax.ShapeDtypeStruct(state.shape, F32)),
        grid=(b,),
        in_specs=[row_spec(qk_w), row_spec(qk_w), row_spec(v_w), row_spec(v_w), st_spec],
        out_specs=(row_spec(v_w), st_spec),
        compiler_params=_params("parallel"),
        name="ret_sample",
    )(q, k, v, rg, state)


def _lane_tile(x, n):
    return x if n == 1 else jnp.concatenate([x] * n, axis=1)


def _flash_update(s, v_ext, m_ref, acc_ref, scale=None):
    p, alpha = _flash_weights(s, m_ref, scale)
    _flash_accumulate(p, alpha, v_ext, acc_ref)


def _flash_weights(s, m_ref, scale=None):
    m_old = m_ref[...]
    m_new = jnp.maximum(m_old, jnp.max(s, axis=-1, keepdims=True))
    m_wide = _lane_tile(m_new, s.shape[1] // LANES)
    if scale is None:
        alpha = jnp.exp(m_old - m_new)
        p = jnp.exp(s - m_wide)
    else:
        alpha = jnp.exp2((m_old - m_new) * (scale * LOG2_E))
        p = jnp.exp2((s - m_wide) * (scale * LOG2_E))
    m_ref[...] = m_new
    return p.astype(BF16), alpha


def _flash_accumulate(p, alpha, v_ext, acc_ref):
    acc_ref[...] = _lane_tile(alpha, 2) * acc_ref[...] + _dot(p, v_ext)


def _flash_reset(m_ref, acc_ref):
    m_ref[...] = jnp.full(m_ref.shape, NEG_INF, F32)
    acc_ref[...] = jnp.zeros(acc_ref.shape, F32)


def _flash_result(acc_ref):
    acc = acc_ref[...]
    return acc[:, :LANES] / acc[:, LANES:]


def _with_ones(v_bf16):
    return jnp.concatenate([v_bf16, jnp.ones(v_bf16.shape, BF16)], axis=1)


def _new_token_mask(rows, keys, t_pad, n_new):
    r = lax.broadcasted_iota(jnp.int32, (rows, keys), 0) % t_pad
    c = lax.broadcasted_iota(jnp.int32, (rows, keys), 1)
    return (c <= r) & (c < n_new)


def _head_rows(ref, idx, h, n_heads, page):
    return ref[idx + (pl.ds(h, page, stride=n_heads), slice(None))]


def _topk_mask(sc, lane, n_sel):
    sel = jnp.zeros(sc.shape, F32)
    for _ in range(n_sel):
        mx = jnp.max(sc, axis=-1, keepdims=True)
        idx = jnp.min(jnp.where(sc == mx, lane, LANES), axis=-1, keepdims=True)
        pick = lane == idx
        sel = jnp.where(pick & (mx > 0.5 * NEG_INF), 1.0, sel)
        sc = jnp.where(pick, BELOW_NEG_INF, sc)
    return sel


def _moba_prompt_kernel(q_ref, k_ref, v_ref, o_ref, kmean_ref, kext_ref, vext_ref, m_ref, acc_ref,
                        *, blk, span_blocks, scale, n_sel):
    i = pl.program_id(1)
    s_len, dh = k_ref.shape
    nb = s_len // blk
    span = span_blocks * blk

    @pl.when(i == 0)
    def _():
        kmean_ref[...] = jnp.zeros_like(kmean_ref)
        kmean_ref[0:nb, :] = jnp.mean(k_ref[...].reshape(nb, blk, dh), axis=1)
        blk_lane = lax.broadcasted_iota(jnp.int32, (blk, LANES), 1)
        for n in range(nb):
            rows = slice(n * blk, (n + 1) * blk)
            kext_ref[rows, 0:LANES] = k_ref[rows, :].astype(BF16)
            kext_ref[rows, LANES:] = jnp.where(blk_lane == n, 1.0, 0.0).astype(BF16)
            vext_ref[rows, :] = _with_ones(v_ref[rows, :].astype(BF16))

    q = q_ref[...]
    lane = lax.broadcasted_iota(jnp.int32, (blk, LANES), 1)
    sc = _dot_nt(q, kmean_ref[...], precision=lax.Precision.HIGHEST)
    sel = _topk_mask(jnp.where(lane < i, sc, NEG_INF), lane, n_sel)
    sel = jnp.where(lane == i, 1.0, sel)
    q_ext = jnp.concatenate([q.astype(BF16), ((1.0 - sel) * NEG_INF).astype(BF16)], axis=1)

    _flash_reset(m_ref, acc_ref)
    last = i // span_blocks
    rows = pl.ds(pl.multiple_of(last * span, span), span)
    s = _dot_nt(q_ext, kext_ref[rows, :])
    q_pos = i * blk + lax.broadcasted_iota(jnp.int32, s.shape, 0)
    k_pos = last * span + lax.broadcasted_iota(jnp.int32, s.shape, 1)
    _flash_update(jnp.where(k_pos <= q_pos, s, NEG_INF), vext_ref[rows, :], m_ref, acc_ref, scale)

    def body(c, carry):
        rows = pl.ds(pl.multiple_of(c * span, span), span)
        _flash_update(_dot_nt(q_ext, kext_ref[rows, :]), vext_ref[rows, :], m_ref, acc_ref, scale)
        return carry

    lax.fori_loop(0, last, body, 0)
    o_ref[...] = _flash_result(acc_ref).astype(BF16)


def _moba_prompt(h_all, cols, n_heads, dh):
    s = h_all.shape[0]
    blk = MOBA_BLOCK
    nb = s // blk
    span_blocks = math.gcd(nb, 4)
    assert s % blk == 0 and nb <= LANES and dh == LANES
    cq, ck, cv = cols["mq"] // dh, cols["mk"] // dh, cols["mv"] // dh
    return pl.pallas_call(
        functools.partial(_moba_prompt_kernel, blk=blk, span_blocks=span_blocks, scale=dh ** -0.5,
                          n_sel=min(MOBA_TOPK, nb - 1)),
        out_shape=jax.ShapeDtypeStruct((s, n_heads * dh), BF16),
        grid=(n_heads, nb),
        in_specs=[pl.BlockSpec((blk, dh), lambda h, i: (i, cq + h)),
                  pl.BlockSpec((s, dh), lambda h, i: (0, ck + h)),
                  pl.BlockSpec((s, dh), lambda h, i: (0, cv + h))],
        out_specs=pl.BlockSpec((blk, dh), lambda h, i: (i, h)),
        scratch_shapes=[pltpu.VMEM((LANES, dh), F32), pltpu.VMEM((s, 2 * LANES), BF16),
                        pltpu.VMEM((s, 2 * LANES), BF16), pltpu.VMEM((blk, LANES), F32),
                        pltpu.VMEM((blk, 2 * LANES), F32)],
        compiler_params=_params("arbitrary", "arbitrary"),
        name="moba_prompt",
    )(h_all, h_all, h_all)


def _moba_sample_kernel(pt_ref, q_ref, *refs, pps, ppb, n_heads, t_pad, n_new, scale, n_sel):
    k_refs, v_refs = refs[:pps], refs[pps:2 * pps]
    kn_ref, vn_ref, o_ref, m_ref, l_ref, sc_ref, part_ref = refs[2 * pps:]
    step = pl.program_id(1)
    page = kn_ref.shape[1] // n_heads
    lane = lax.broadcasted_iota(jnp.int32, (t_pad, LANES), 1)

    @pl.when(step == 0)
    def _():
        m_ref[...] = jnp.full(m_ref.shape, NEG_INF, F32)
        l_ref[...] = jnp.zeros(l_ref.shape, F32)
        sc_ref[...] = jnp.full(sc_ref.shape, NEG_INF, F32)

    blocks = pps // ppb
    blk = ppb * page
    s_raw = [_dot_nt(q_ref[0, h], jnp.concatenate(
        [_head_rows(r, (0, 0), h, n_heads, page) for r in k_refs], 0).astype(BF16)) for h in range(n_heads)]
    m_new, l_new, sc_new = ([m_ref[h] for h in range(n_heads)], [l_ref[h] for h in range(n_heads)],
                            [sc_ref[h] for h in range(n_heads)])
    probs = []
    for h in range(n_heads):
        for j in range(blocks):
            s_blk = s_raw[h][:, j * blk:(j + 1) * blk]
            m_n = jnp.max(s_blk, axis=-1, keepdims=True) * scale
            probs.append(jnp.exp(s_blk * scale - m_n).astype(BF16))
            here = lane == step * blocks + j
            m_new[h] = jnp.where(here, m_n, m_new[h])
            sc_new[h] = jnp.where(here, jnp.mean(s_blk, axis=-1, keepdims=True), sc_new[h])
    parts = []
    for h in range(n_heads):
        for j in range(blocks):
            vb = jnp.concatenate([_head_rows(v_refs[g], (0, 0), h, n_heads, page)
                                  for g in range(j * ppb, (j + 1) * ppb)], 0).astype(BF16)
            part = _dot(probs[h * blocks + j], _with_ones(vb))
            parts.append(part[:, :LANES])
            l_new[h] = jnp.where(lane == step * blocks + j, part[:, LANES:], l_new[h])
    part_ref[step] = jnp.concatenate(parts, axis=0)
    for h in range(n_heads):
        m_ref[h], l_ref[h], sc_ref[h] = m_new[h], l_new[h], sc_new[h]

    @pl.when(step == pl.num_programs(1) - 1)
    def _():
        for h in range(n_heads):
            qb = q_ref[0, h]
            sel = _topk_mask(sc_ref[h], lane, n_sel) > 0.5
            s_own = _dot_nt(qb, _head_rows(kn_ref, (0,), h, n_heads, page).astype(BF16)) * scale
            s_own = jnp.where(_new_token_mask(t_pad, page, t_pad, n_new), s_own, NEG_INF)
            m_blocks = m_ref[h]
            m_all = jnp.maximum(jnp.max(jnp.where(sel, m_blocks, NEG_INF), axis=-1, keepdims=True),
                                jnp.max(s_own, axis=-1, keepdims=True))
            p_own = jnp.exp(s_own - m_all)
            w = jnp.where(sel, jnp.exp(m_blocks - m_all), 0.0)
            l_all = jnp.sum(w * l_ref[h], axis=-1, keepdims=True) + jnp.sum(p_own, axis=-1, keepdims=True)
            acc = _dot(p_own.astype(BF16), _head_rows(vn_ref, (0,), h, n_heads, page).astype(BF16))
            for b in range(part_ref.shape[0] * blocks):
                row = (h * blocks + b % blocks) * t_pad
                acc = acc + w[:, b:b + 1] * part_ref[b // blocks, row:row + t_pad, :]
            o_ref[0, :, h * LANES:(h + 1) * LANES] = (acc / l_all).astype(BF16)


def _moba_sample(layer, q, k_new, v_new, cache_k, cache_v, page_table, n_heads, n_new, pps):
    b, _, t_pad, dh = q.shape
    page = cache_k.shape[2] // n_heads
    n_pages = page_table.shape[1]
    ppb = MOBA_BLOCK // page
    nb = n_pages // ppb
    assert ppb * page == MOBA_BLOCK and n_pages % pps == 0 and pps % ppb == 0 and nb <= LANES and dh == LANES

    def page_spec(g):
        return pl.BlockSpec((1, 1, page * n_heads, dh), lambda i, n, pt: (layer, pt[i, n * pps + g], 0, 0))

    new_spec = pl.BlockSpec((1, page * n_heads, dh), lambda i, n, pt: (i, 0, 0))
    stat = pltpu.VMEM((n_heads, t_pad, LANES), F32)
    grid_spec = pltpu.PrefetchScalarGridSpec(
        num_scalar_prefetch=1,
        grid=(b, n_pages // pps),
        in_specs=[pl.BlockSpec((1, n_heads, t_pad, dh), lambda i, n, pt: (i, 0, 0, 0))]
                 + [page_spec(g) for g in range(pps)] * 2 + [new_spec, new_spec],
        out_specs=pl.BlockSpec((1, t_pad, n_heads * dh), lambda i, n, pt: (i, 0, 0)),
        scratch_shapes=[stat, stat, stat,
                        pltpu.VMEM((n_pages // pps, (pps // ppb) * n_heads * t_pad, dh), F32)],
    )
    return pl.pallas_call(
        functools.partial(_moba_sample_kernel, pps=pps, ppb=ppb, n_heads=n_heads, t_pad=t_pad, n_new=n_new,
                          scale=dh ** -0.5, n_sel=min(MOBA_TOPK, nb)),
        out_shape=jax.ShapeDtypeStruct((b, t_pad, n_heads * dh), BF16),
        grid_spec=grid_spec,
        compiler_params=_params("parallel", "arbitrary"),
        name="moba_sample",
    )(page_table, q, *([cache_k] * pps), *([cache_v] * pps), k_new, v_new)


def _diff_lambda(lqk, lam_init):
    s1 = jnp.sum(lqk[0:1] * lqk[1:2], axis=-1, keepdims=True)
    s2 = jnp.sum(lqk[2:3] * lqk[3:4], axis=-1, keepdims=True)
    return jnp.exp(s1) - jnp.exp(s2) + lam_init


def _diff_finish(o1, o2, lqk, g, lam_init):
    od = o1 - _diff_lambda(lqk, lam_init) * o2
    od = od * lax.rsqrt(jnp.mean(od * od, axis=-1, keepdims=True) + NORM_EPS) * g
    return od * (1.0 - lam_init)


def _split_sub_heads(q):
    first = lax.broadcasted_iota(jnp.int32, q.shape, q.ndim - 1) < LANES // 2
    return jnp.concatenate([jnp.where(first, q, 0.0), jnp.where(first, 0.0, q)], axis=q.ndim - 2)


def _diff_prompt_kernel(q_ref, k_ref, v_ref, g_ref, lqk_ref, o_ref, kb_ref, vext_ref, m_ref, acc_ref,
                        *, tile, scale, lam_init):
    i = pl.program_id(1)
    s_len = k_ref.shape[0]

    @pl.when(i == 0)
    def _():
        for c in range(s_len // tile):
            rows = slice(c * tile, (c + 1) * tile)
            kb_ref[rows, :] = k_ref[rows, :].astype(BF16)
            vext_ref[rows, :] = _with_ones(v_ref[rows, :].astype(BF16))

    fold = _is_pow2(scale)
    qs = _split_sub_heads(q_ref[...])
    qs = (qs * scale if fold else qs).astype(BF16)

    def scores(rows):
        s = _dot_nt(qs, kb_ref[rows, :])
        return s if fold else s * scale

    _flash_reset(m_ref, acc_ref)
    diag = pl.ds(pl.multiple_of(i * tile, tile), tile)
    s = scores(diag)
    row = lax.broadcasted_iota(jnp.int32, s.shape, 0) % tile
    col = lax.broadcasted_iota(jnp.int32, s.shape, 1)
    _flash_update(jnp.where(col <= row, s, NEG_INF), vext_ref[diag, :], m_ref, acc_ref)

    def body(n, carry):
        past = pl.ds(pl.multiple_of(n * tile, tile), tile)
        _flash_update(scores(past), vext_ref[past, :], m_ref, acc_ref)
        return carry

    lax.fori_loop(0, i, body, 0)
    o = _flash_result(acc_ref)
    o_ref[...] = _diff_finish(o[:tile], o[tile:], lqk_ref[...], g_ref[...], lam_init).astype(BF16)


def _diff_prompt(h_all, cols, lqk, subln_g, lam_init, n_heads, dh, tile):
    s = h_all.shape[0]
    w = 2 * dh
    assert w == LANES and s % tile == 0
    cq, ck, cv = cols["dq"] // w, cols["dk"] // w, cols["dv"] // w
    return pl.pallas_call(
        functools.partial(_diff_prompt_kernel, tile=tile, scale=dh ** -0.5, lam_init=lam_init),
        out_shape=jax.ShapeDtypeStruct((s, n_heads * w), BF16),
        grid=(n_heads, s // tile),
        in_specs=[pl.BlockSpec((tile, w), lambda h, i: (i, cq + h)),
                  pl.BlockSpec((s, w), lambda h, i: (0, ck + h)),
                  pl.BlockSpec((s, w), lambda h, i: (0, cv + h)),
                  pl.BlockSpec((1, w), lambda h, i: (0, 0)),
                  pl.BlockSpec(lqk.shape, lambda h, i: (0, 0))],
        out_specs=pl.BlockSpec((tile, w), lambda h, i: (i, h)),
        scratch_shapes=[pltpu.VMEM((s, w), BF16), pltpu.VMEM((s, 2 * LANES), BF16),
                        pltpu.VMEM((2 * tile, LANES), F32), pltpu.VMEM((2 * tile, 2 * LANES), F32)],
        compiler_params=_params("arbitrary", "arbitrary"),
        name="diff_prompt",
    )(h_all, h_all, h_all, subln_g.reshape(1, w), lqk)


def _diff_sample_kernel(pt_ref, q_ref, *refs, pps, n_heads, t_pad, n_new, lam_init):
    k_refs, v_refs = refs[:pps], refs[pps:2 * pps]
    kn_ref, vn_ref, g_ref, lqk_ref, o_ref, m_ref, acc_ref = refs[2 * pps:]
    step = pl.program_id(1)
    page = kn_ref.shape[2]

    def head_dims(h):
        return slice(h * LANES, (h + 1) * LANES)

    @pl.when(step == 0)
    def _():
        _flash_reset(m_ref, acc_ref)
        for h in range(n_heads):
            s = _dot(q_ref[0, h], kn_ref[0, head_dims(h), :].astype(BF16))
            s = jnp.where(_new_token_mask(2 * t_pad, page, t_pad, n_new), s, NEG_INF)
            _flash_update(s, _with_ones(_head_rows(vn_ref, (0,), h, n_heads, page).astype(BF16)),
                          m_ref.at[h], acc_ref.at[h])

    scores = [_dot(q_ref[0, h], jnp.concatenate([r[0, 0, head_dims(h), :] for r in k_refs], 1).astype(BF16))
              for h in range(n_heads)]
    weights = [_flash_weights(scores[h], m_ref.at[h]) for h in range(n_heads)]
    for h in range(n_heads):
        vb = jnp.concatenate([_head_rows(r, (0, 0), h, n_heads, page) for r in v_refs], 0).astype(BF16)
        _flash_accumulate(*weights[h], _with_ones(vb), acc_ref.at[h])

    @pl.when(step == pl.num_programs(1) - 1)
    def _():
        for h in range(n_heads):
            o = _flash_result(acc_ref.at[h])
            o_ref[0, :, h * LANES:(h + 1) * LANES] = _diff_finish(
                o[:t_pad], o[t_pad:], lqk_ref[...], g_ref[...], lam_init).astype(BF16)


def _diff_sample(layer, q, k_new, v_new, cache_k, cache_v, page_table, lqk, subln_g, lam_init,
                 n_heads, n_new, pps):
    b, _, rows, w = q.shape
    t_pad = rows // 2
    page = cache_k.shape[3]
    n_pages = page_table.shape[1]
    assert n_pages % pps == 0 and w == LANES and page == LANES

    def page_spec(g, shape):
        return pl.BlockSpec((1, 1) + shape, lambda i, n, pt: (layer, pt[i, n * pps + g], 0, 0))

    def new_spec(shape):
        return pl.BlockSpec((1,) + shape, lambda i, n, pt: (i, 0, 0))

    k_shape, v_shape = (n_heads * w, page), (page * n_heads, w)
    grid_spec = pltpu.PrefetchScalarGridSpec(
        num_scalar_prefetch=1,
        grid=(b, n_pages // pps),
        in_specs=[pl.BlockSpec((1, n_heads, rows, w), lambda i, n, pt: (i, 0, 0, 0))]
                 + [page_spec(g, k_shape) for g in range(pps)] + [page_spec(g, v_shape) for g in range(pps)]
                 + [new_spec(k_shape), new_spec(v_shape),
                    pl.BlockSpec((1, LANES), lambda i, n, pt: (0, 0)),
                    pl.BlockSpec(lqk.shape, lambda i, n, pt: (0, 0))],
        out_specs=pl.BlockSpec((1, t_pad, n_heads * w), lambda i, n, pt: (i, 0, 0)),
        scratch_shapes=[pltpu.VMEM((n_heads, rows, LANES), F32), pltpu.VMEM((n_heads, rows, 2 * LANES), F32)],
    )
    return pl.pallas_call(
        functools.partial(_diff_sample_kernel, pps=pps, n_heads=n_heads, t_pad=t_pad, n_new=n_new,
                          lam_init=lam_init),
        out_shape=jax.ShapeDtypeStruct((b, t_pad, n_heads * w), BF16),
        grid_spec=grid_spec,
        compiler_params=_params("parallel", "arbitrary"),
        name="diff_sample",
    )(page_table, q, *([cache_k] * pps), *([cache_v] * pps), k_new, v_new, subln_g.reshape(1, LANES), lqk)


def _merge_kernel(x_ref, yr_ref, om_ref, od_ref, gates_ref, wr_ref, wm_ref, wd_ref, wo_ref, o_ref):
    d = x_ref.shape[1]
    mix = (jax.nn.sigmoid(gates_ref[:, 0:d]) * _dot(yr_ref[...], wr_ref[...])
           + jax.nn.sigmoid(gates_ref[:, d:2 * d]) * _dot(om_ref[...], wm_ref[...])
           + jax.nn.sigmoid(gates_ref[:, 2 * d:3 * d]) * _dot(od_ref[...], wd_ref[...]))
    o_ref[...] = x_ref[...] + _dot(mix.astype(BF16), wo_ref[...])


def _merge(x, y_ret, o_moba, o_diff, h_all, gate_col, w_ret, w_moba, w_diff, w_out, tm):
    m, d = x.shape
    assert gate_col % (3 * d) == 0

    def rows(w):
        return pl.BlockSpec((tm, w), lambda i: (i, 0))

    def whole(arr):
        return pl.BlockSpec(arr.shape, lambda i: (0, 0), pipeline_mode=pl.Buffered(1))

    return pl.pallas_call(
        _merge_kernel,
        out_shape=jax.ShapeDtypeStruct((m, d), F32),
        grid=(m // tm,),
        in_specs=[rows(d), rows(y_ret.shape[1]), rows(o_moba.shape[1]), rows(o_diff.shape[1]),
                  pl.BlockSpec((tm, 3 * d), lambda i: (i, gate_col // (3 * d))),
                  whole(w_ret), whole(w_moba), whole(w_diff), whole(w_out)],
        out_specs=rows(d),
        compiler_params=_params("parallel"),
        name="merge_out",
    )(x, y_ret, o_moba, o_diff, h_all, w_ret, w_moba, w_diff, w_out)


def _ffn_kernel(x_ref, g_ref, w1_ref, w2_ref, fg_ref, o_ref, xn_ref, *, final_norm):
    t = pl.program_id(1)

    @pl.when(t == 0)
    def _():
        x = x_ref[...]
        xn_ref[...] = _rms(x, g_ref[...]).astype(BF16)
        o_ref[...] = x

    hid = jnp.square(jnp.maximum(_dot(xn_ref[...], w1_ref[...]), 0.0))
    o_ref[...] += _dot(hid.astype(BF16), w2_ref[...])

    if final_norm:
        @pl.when(t == pl.num_programs(1) - 1)
        def _():
            o_ref[...] = _rms(o_ref[...], fg_ref[...])


def _ffn(x, g, w1, w2, final_g, final_norm, tm, tf):
    m, d = x.shape
    f = w1.shape[1]
    return pl.pallas_call(
        functools.partial(_ffn_kernel, final_norm=final_norm),
        out_shape=jax.ShapeDtypeStruct((m, d), F32),
        grid=(m // tm, f // tf),
        in_specs=[pl.BlockSpec((tm, d), lambda i, t: (i, 0)),
                  pl.BlockSpec((1, d), lambda i, t: (0, 0)),
                  pl.BlockSpec((d, tf), lambda i, t: (0, t)),
                  pl.BlockSpec((tf, d), lambda i, t: (t, 0)),
                  pl.BlockSpec((1, d), lambda i, t: (0, 0))],
        out_specs=pl.BlockSpec((tm, d), lambda i, t: (i, 0)),
        scratch_shapes=[pltpu.VMEM((tm, d), BF16)],
        compiler_params=_params("parallel", "arbitrary"),
        name="ffn",
    )(x, g.reshape(1, d), w1, w2, final_g.reshape(1, d))


def _pad_rows(x, rows):
    return jnp.pad(x, ((0, 0), (0, rows - x.shape[1]), (0, 0)))


def _per_head(x, n_heads, t_pad):
    b = x.shape[0]
    return _pad_rows(x, t_pad).reshape(b, t_pad, n_heads, LANES).transpose(0, 2, 1, 3)


def _as_page(x, n_heads, page):
    return _pad_rows(x, page).reshape(x.shape[0], page * n_heads, LANES)


def kernel(x_prompt, x_sample, cache_moba_k, cache_moba_v, cache_diff_k, cache_diff_v, state_ret, page_table, norm1_g, w_in, diff_lq1, diff_lk1, diff_lq2, diff_lk2, diff_subln_g, w_up_ret, w_up_moba, w_up_diff, w_out, norm2_g, w_ff1, w_ff2, final_g):
    bp, s_len, d = x_prompt.shape
    bs, t_new, _ = x_sample.shape
    depth, n_pool, page, moba_h, moba_dh = cache_moba_k.shape
    diff_h, diff_dh = cache_diff_k.shape[3], cache_diff_k.shape[5]
    ret_h, ret_dk, ret_dv = state_ret.shape[2:]
    n_pages = page_table.shape[1]
    past_len = n_pages * page
    assert bp == 1 and ret_dk == LANES and moba_dh == LANES and 2 * diff_dh == LANES
    assert t_new <= SUBLANES and past_len % MOBA_BLOCK == 0

    ret_qk_w, ret_v_w = ret_h * ret_dk, ret_h * ret_dv
    moba_w, diff_w = moba_h * moba_dh, diff_h * 2 * diff_dh
    names = ["rq", "rk", "rv", "rg", "mq", "mk", "mv", "dq", "dk", "dv", "gr", "gm", "gd"]
    widths = [ret_qk_w, ret_qk_w, ret_v_w, ret_v_w, moba_w, moba_w, moba_w, diff_w, diff_w, diff_w, d, d, d]
    cols, off = {}, 0
    for nm, wd in zip(names, widths):
        cols[nm] = off
        off += wd
    in_w = off
    assert w_in.shape == (depth, d, in_w)

    tn = 512
    kinds = {"rq": EPI_ROPE128, "rk": EPI_ROPE128_SCALED, "mq": EPI_ROPE128, "mk": EPI_ROPE128,
             "dq": EPI_ROPE64, "dk": EPI_ROPE64}
    tile_kinds = []
    for nm, wd in zip(names, widths):
        assert wd % tn == 0
        tile_kinds += [kinds.get(nm, EPI_NONE)] * (wd // tn)
    tile_kinds = tuple(tile_kinds)

    m_s = bs * t_new
    xp = x_prompt.reshape(s_len, d)
    xs = x_sample.reshape(m_s, d)
    tables_p = _rope_tables(jnp.arange(s_len))
    tables_s = _rope_tables(jnp.tile(past_len + jnp.arange(t_new), bs))
    tm_p = min(1024, s_len)
    tm_ffn = min(512, s_len)
    tm_merge = min(256, s_len)
    tf = min(512, w_ff1.shape[-1])
    diff_tile = min(512, s_len)
    ret_chunk = min(256, s_len)
    t_pad = SUBLANES
    pps = min(8, n_pages)

    ck_moba = cache_moba_k.reshape(depth, n_pool, page * moba_h, moba_dh)
    cv_moba = cache_moba_v.reshape(depth, n_pool, page * moba_h, moba_dh)
    ck_diff = cache_diff_k.transpose(0, 1, 3, 4, 5, 2).reshape(depth, n_pool, diff_w, page)
    cv_diff = cache_diff_v.reshape(depth, n_pool, page * diff_h, 2 * diff_dh)

    outs = {k: [] for k in ["pmk", "pmv", "pdk", "pdv", "pst", "smk", "smv", "sdk", "sdv", "sst"]}
    for l in range(depth):
        lam_init = 0.8 - 0.6 * math.exp(-0.3 * l)
        lqk = jnp.stack([diff_lq1[l], diff_lk1[l], diff_lq2[l], diff_lk2[l]])
        w_in_l = w_in[l].astype(BF16)
        w_ret_l, w_moba_l, w_diff_l = (w_up_ret[l].astype(BF16), w_up_moba[l].astype(BF16),
                                       w_up_diff[l].astype(BF16))
        w_out_l, w_ff1_l, w_ff2_l = w_out[l].astype(BF16), w_ff1[l].astype(BF16), w_ff2[l].astype(BF16)
        last = l == depth - 1
        k_scale = ret_dk ** -0.5

        def sec(h_all, nm, wd):
            return h_all[:, cols[nm]:cols[nm] + wd]

        hp = _in_proj(xp, norm1_g[l], w_in_l, tables_p, tile_kinds, k_scale, tm_p, tn)
        y_ret, s_fin = _ret_prompt(hp, cols, ret_h, ret_dk, ret_dv, ret_chunk)
        o_moba = _moba_prompt(hp, cols, moba_h, moba_dh)
        o_diff = _diff_prompt(hp, cols, lqk, diff_subln_g[l], lam_init, diff_h, diff_dh, diff_tile)
        x1 = _merge(xp, y_ret, o_moba, o_diff, hp, cols["gr"], w_ret_l, w_moba_l, w_diff_l, w_out_l, tm_merge)
        xp = _ffn(x1, norm2_g[l], w_ff1_l, w_ff2_l, final_g, last, tm_ffn, tf)
        outs["pmk"].append(sec(hp, "mk", moba_w).reshape(bp, s_len, moba_h, moba_dh))
        outs["pmv"].append(sec(hp, "mv", moba_w).reshape(bp, s_len, moba_h, moba_dh))
        outs["pdk"].append(sec(hp, "dk", diff_w).reshape(bp, s_len, diff_h, 2, diff_dh))
        outs["pdv"].append(sec(hp, "dv", diff_w).reshape(bp, s_len, diff_h, 2 * diff_dh))
        outs["pst"].append(s_fin.reshape(bp, ret_h, ret_dk, ret_dv))

        hs = _in_proj(xs, norm1_g[l], w_in_l, tables_s, tile_kinds, k_scale, m_s, tn)

        def sec3(nm, wd):
            return sec(hs, nm, wd).reshape(bs, t_new, wd)

        y_ret_s, s_new = _ret_sample(
            _pad_rows(sec3("rq", ret_qk_w), t_pad), _pad_rows(sec3("rk", ret_qk_w), t_pad),
            _pad_rows(sec3("rv", ret_v_w), t_pad), _pad_rows(sec3("rg", ret_v_w), t_pad),
            state_ret[l], ret_h, ret_dk, ret_dv, t_new)
        mk_s, mv_s = sec3("mk", moba_w), sec3("mv", moba_w)
        o_moba_s = _moba_sample(l, _per_head(sec3("mq", moba_w), moba_h, t_pad).astype(BF16),
                                _as_page(mk_s, moba_h, page), _as_page(mv_s, moba_h, page),
                                ck_moba, cv_moba, page_table, moba_h, t_new, pps)
        dk_s, dv_s = sec3("dk", diff_w), sec3("dv", diff_w)
        diff_scale = diff_dh ** -0.5
        assert _is_pow2(diff_scale)
        dq_s = _split_sub_heads(_per_head(sec3("dq", diff_w) * diff_scale, diff_h, t_pad)).astype(BF16)
        o_diff_s = _diff_sample(l, dq_s, _pad_rows(dk_s, page).transpose(0, 2, 1), _as_page(dv_s, diff_h, page),
                                ck_diff, cv_diff, page_table, lqk, diff_subln_g[l], lam_init,
                                diff_h, t_new, pps)

        def unpad(o):
            return o[:, :t_new].reshape(m_s, o.shape[-1])

        x1s = _merge(xs, unpad(y_ret_s), unpad(o_moba_s), unpad(o_diff_s), hs, cols["gr"],
                     w_ret_l, w_moba_l, w_diff_l, w_out_l, m_s)
        xs = _ffn(x1s, norm2_g[l], w_ff1_l, w_ff2_l, final_g, last, m_s, tf)
        outs["smk"].append(mk_s.reshape(bs, t_new, moba_h, moba_dh))
        outs["smv"].append(mv_s.reshape(bs, t_new, moba_h, moba_dh))
        outs["sdk"].append(dk_s.reshape(bs, t_new, diff_h, 2, diff_dh))
        outs["sdv"].append(dv_s.reshape(bs, t_new, diff_h, 2 * diff_dh))
        outs["sst"].append(s_new)

    st = {k: jnp.stack(v) for k, v in outs.items()}
    return (xp.reshape(bp, s_len, d), xs.reshape(bs, t_new, d),
            st["pmk"], st["pmv"], st["pdk"], st["pdv"], st["pst"],
            st["smk"], st["smv"], st["sdk"], st["sdv"], st["sst"])
```

```python
import functools
import math

import numpy as np
import jax
import jax.numpy as jnp
from jax import lax
from jax.experimental import pallas as pl
from jax.experimental.pallas import tpu as pltpu

NORM_EPS = 1e-6
NEG_INF = -1e30
BELOW_NEG_INF = -3e38
ROPE_THETA = 10000.0
MOBA_BLOCK = 256
MOBA_TOPK = 3
LANES = 128
SUBLANES = 8
VMEM_LIMIT_BYTES = 48 * 1024 * 1024
LOG2_E = 1.4426950408889634

BF16 = jnp.bfloat16
F32 = jnp.float32


def _dot(a, b):
    return jnp.dot(a, b, preferred_element_type=F32)


def _dot_nt(a, b, precision=None):
    return lax.dot_general(a, b, (((1,), (1,)), ((), ())), precision=precision,
                           preferred_element_type=F32)


def _dot_tn(a, b):
    return lax.dot_general(a, b, (((0,), (0,)), ((), ())), preferred_element_type=F32)


def _params(*sem):
    return pltpu.CompilerParams(dimension_semantics=sem, vmem_limit_bytes=VMEM_LIMIT_BYTES)


def _rms(x, g):
    return x * lax.rsqrt(jnp.mean(x * x, axis=-1, keepdims=True) + NORM_EPS) * g


def _is_pow2(v):
    m, _ = math.frexp(v)
    return m == 0.5


EPI_NONE, EPI_ROPE128, EPI_ROPE128_SCALED, EPI_ROPE64 = 0, 1, 2, 3


def _in_proj_kernel(x_ref, g_ref, w_ref, ca_ref, sa_ref, cb_ref, sb1_ref, sb2_ref, o_ref, xn_ref,
                    *, tile_kinds, k_scale):
    j = pl.program_id(1)

    @pl.when(j == 0)
    def _():
        xn_ref[...] = _rms(x_ref[...], g_ref[...]).astype(BF16)

    o_ref[...] = _dot(xn_ref[...], w_ref[...])
    n_groups = o_ref.shape[1] // LANES

    def among(kind):
        js = [t for t, kd in enumerate(tile_kinds) if kd == kind]
        cond = j == js[0]
        for t in js[1:]:
            cond = cond | (j == t)
        return cond

    def rope128(scale):
        ca, sa = ca_ref[...], sa_ref[...]
        for c in range(n_groups):
            seg = o_ref[:, c * LANES:(c + 1) * LANES]
            out = seg * ca + pltpu.roll(seg, LANES // 2, axis=1) * sa
            o_ref[:, c * LANES:(c + 1) * LANES] = out if scale is None else out * scale

    @pl.when(among(EPI_ROPE128))
    def _():
        rope128(None)

    @pl.when(among(EPI_ROPE128_SCALED))
    def _():
        rope128(k_scale)

    @pl.when(among(EPI_ROPE64))
    def _():
        cb, sb1, sb2 = cb_ref[...], sb1_ref[...], sb2_ref[...]
        for c in range(n_groups):
            seg = o_ref[:, c * LANES:(c + 1) * LANES]
            up = pltpu.roll(seg, LANES - LANES // 4, axis=1)
            down = pltpu.roll(seg, LANES // 4, axis=1)
            o_ref[:, c * LANES:(c + 1) * LANES] = seg * cb + up * sb1 + down * sb2


def _rope_tables(pos):
    f32 = np.float32
    pos = np.asarray(pos, np.float64)[:, None]
    half_a, half_b = LANES // 2, LANES // 4
    inv_a = ROPE_THETA ** (-np.arange(half_a, dtype=np.float64) / half_a)
    inv_b = ROPE_THETA ** (-np.arange(half_b, dtype=np.float64) / half_b)
    ang_a = pos * inv_a[None, :]
    ang_b = pos * inv_b[None, :]
    cos_a, sin_a = np.cos(ang_a), np.sin(ang_a)
    cos_b, sin_b = np.cos(ang_b), np.sin(ang_b)
    zero_b = np.zeros_like(sin_b)
    ca = np.concatenate([cos_a, cos_a], -1)
    sa = np.concatenate([-sin_a, sin_a], -1)
    cb = np.concatenate([cos_b] * 4, -1)
    sb1 = np.concatenate([-sin_b, zero_b] * 2, -1)
    sb2 = np.concatenate([zero_b, sin_b] * 2, -1)
    return tuple(jnp.asarray(t.astype(f32)) for t in (ca, sa, cb, sb1, sb2))


def _in_proj(x, g, w_all, layer, tables, tile_kinds, k_scale, tm, tn):
    m, d = x.shape
    n = w_all.shape[2]
    tab_spec = pl.BlockSpec((tm, LANES), lambda i, j: (i, 0))
    return pl.pallas_call(
        functools.partial(_in_proj_kernel, tile_kinds=tile_kinds, k_scale=k_scale),
        out_shape=jax.ShapeDtypeStruct((m, n), F32),
        grid=(m // tm, n // tn),
        in_specs=[pl.BlockSpec((tm, d), lambda i, j: (i, 0)),
                  pl.BlockSpec((1, d), lambda i, j: (0, 0)),
                  pl.BlockSpec((None, d, tn), lambda i, j: (layer, 0, j)),
                  tab_spec, tab_spec, tab_spec, tab_spec, tab_spec],
        out_specs=pl.BlockSpec((tm, tn), lambda i, j: (i, j)),
        scratch_shapes=[pltpu.VMEM((tm, d), BF16)],
        compiler_params=_params("parallel", "arbitrary"),
        name="in_proj",
    )(x, g.reshape(1, d), w_all, *tables)


def _ret_log_g(h):
    return math.log(1.0 - 2.0 ** (-5.0 - h))


def _ret_heads(q_all, k_all, v_all, rg_all, get_state, set_state, write_y, *, n_heads, dk, dv, chunk):
    rows = q_all.shape[0]
    ri = lax.broadcasted_iota(jnp.int32, (rows, rows), 0)
    ci = lax.broadcasted_iota(jnp.int32, (rows, rows), 1)
    diff = (ri - ci).astype(F32)
    pos = lax.broadcasted_iota(jnp.int32, (rows, 1), 0).astype(F32)
    for h in range(n_heads):
        lg = _ret_log_g(h)
        q = q_all[:, h * dk:(h + 1) * dk]
        k = k_all[:, h * dk:(h + 1) * dk]
        vb = v_all[:, h * dv:(h + 1) * dv].astype(BF16)
        qb = q.astype(BF16)
        dmask = jnp.where(diff >= 0, jnp.exp(jnp.maximum(diff, 0.0) * lg), 0.0)
        a = _dot_nt(qb, k.astype(BF16)) * dmask
        state = get_state(h)
        o = _dot(a.astype(BF16), vb) + _dot(qb, state.astype(BF16)) * jnp.exp((pos + 1.0) * lg)
        kd = (k * jnp.exp((chunk - 1.0 - pos) * lg)).astype(BF16)
        set_state(h, state * math.exp(chunk * lg) + _dot_tn(kd, vb))
        yn = o * lax.rsqrt(jnp.mean(o * o, axis=-1, keepdims=True) + NORM_EPS)
        rg = rg_all[:, h * dv:(h + 1) * dv]
        write_y(h, (yn * (rg * jax.nn.sigmoid(rg))).astype(BF16))


def _ret_prompt_kernel(q_ref, k_ref, v_ref, rg_ref, y_ref, s_ref, *, n_heads, dk, dv, chunk):
    @pl.when(pl.program_id(0) == 0)
    def _():
        s_ref[...] = jnp.zeros_like(s_ref)

    def set_state(h, val):
        s_ref[h] = val

    def write_y(h, val):
        y_ref[:, h * dv:(h + 1) * dv] = val

    _ret_heads(q_ref[...], k_ref[...], v_ref[...], rg_ref[...], lambda h: s_ref[h], set_state, write_y,
               n_heads=n_heads, dk=dk, dv=dv, chunk=chunk)


def _ret_prompt(h_all, cols, n_heads, dk, dv, chunk):
    s = h_all.shape[0]
    qk_w, v_w = n_heads * dk, n_heads * dv
    return pl.pallas_call(
        functools.partial(_ret_prompt_kernel, n_heads=n_heads, dk=dk, dv=dv, chunk=chunk),
        out_shape=(jax.ShapeDtypeStruct((s, v_w), BF16),
                   jax.ShapeDtypeStruct((n_heads, dk, dv), F32)),
        grid=(s // chunk,),
        in_specs=[pl.BlockSpec((chunk, qk_w), lambda c: (c, cols["rq"] // qk_w)),
                  pl.BlockSpec((chunk, qk_w), lambda c: (c, cols["rk"] // qk_w)),
                  pl.BlockSpec((chunk, v_w), lambda c: (c, cols["rv"] // v_w)),
                  pl.BlockSpec((chunk, v_w), lambda c: (c, cols["rg"] // v_w))],
        out_specs=(pl.BlockSpec((chunk, v_w), lambda c: (c, 0)),
                   pl.BlockSpec((n_heads, dk, dv), lambda c: (0, 0, 0))),
        compiler_params=_params("arbitrary"),
        name="ret_prompt",
    )(h_all, h_all, h_all, h_all)


def _ret_sample_kernel(q_ref, k_ref, v_ref, rg_ref, s0_ref, y_ref, s_ref, *, n_heads, dk, dv, chunk):
    def set_state(h, val):
        s_ref[0, h] = val

    def write_y(h, val):
        y_ref[0, :, h * dv:(h + 1) * dv] = val

    _ret_heads(q_ref[0], k_ref[0], v_ref[0], rg_ref[0], lambda h: s0_ref[0, h], set_state, write_y,
               n_heads=n_heads, dk=dk, dv=dv, chunk=chunk)


def _ret_sample(q, k, v, rg, state, n_heads, dk, dv, chunk):
    b, rows, _ = q.shape
    qk_w, v_w = n_heads * dk, n_heads * dv

    def row_spec(w):
        return pl.BlockSpec((1, rows, w), lambda i: (i, 0, 0))

    st_spec = pl.BlockSpec((1, n_heads, dk, dv), lambda i: (i, 0, 0, 0))
    return pl.pallas_call(
        functools.partial(_ret_sample_kernel, n_heads=n_heads, dk=dk, dv=dv, chunk=chunk),
        out_shape=(jax.ShapeDtypeStruct((b, rows, v_w), BF16),
                   jax.ShapeDtypeStruct(state.shape, F32)),
        grid=(b,),
        in_specs=[row_spec(qk_w), row_spec(qk_w), row_spec(v_w), row_spec(v_w), st_spec],
        out_specs=(row_spec(v_w), st_spec),
        compiler_params=_params("parallel"),
        name="ret_sample",
    )(q, k, v, rg, state)


def _lane_tile(x, n):
    return x if n == 1 else jnp.concatenate([x] * n, axis=1)


def _flash_update(s, v_ext, m_ref, acc_ref, scale=None):
    p, alpha = _flash_weights(s, m_ref, scale)
    _flash_accumulate(p, alpha, v_ext, acc_ref)


def _flash_weights(s, m_ref, scale=None):
    m_old = m_ref[...]
    m_new = jnp.maximum(m_old, jnp.max(s, axis=-1, keepdims=True))
    m_wide = _lane_tile(m_new, s.shape[1] // LANES)
    if scale is None:
        alpha = jnp.exp(m_old - m_new)
        p = jnp.exp(s - m_wide)
    else:
        alpha = jnp.exp2((m_old - m_new) * (scale * LOG2_E))
        p = jnp.exp2((s - m_wide) * (scale * LOG2_E))
    m_ref[...] = m_new
    return p.astype(BF16), alpha


def _flash_accumulate(p, alpha, v_ext, acc_ref):
    acc_ref[...] = _lane_tile(alpha, 2) * acc_ref[...] + _dot(p, v_ext)


def _flash_over_tiles(n_tiles, tile_rows, score_fn, value_fn, m_ref, acc_ref, scale=None):
    def pair(c, carry):
        rows_a, rows_b = tile_rows(2 * c), tile_rows(2 * c + 1)
        s_a, s_b = score_fn(rows_a), score_fn(rows_b)
        _flash_update(s_a, value_fn(rows_a), m_ref, acc_ref, scale)
        _flash_update(s_b, value_fn(rows_b), m_ref, acc_ref, scale)
        return carry

    lax.fori_loop(0, n_tiles // 2, pair, 0)

    @pl.when(n_tiles % 2 == 1)
    def _():
        rows = tile_rows(n_tiles - 1)
        _flash_update(score_fn(rows), value_fn(rows), m_ref, acc_ref, scale)


def _flash_reset(m_ref, acc_ref):
    m_ref[...] = jnp.full(m_ref.shape, NEG_INF, F32)
    acc_ref[...] = jnp.zeros(acc_ref.shape, F32)


def _flash_result(acc_ref):
    acc = acc_ref[...]
    return acc[:, :LANES] / acc[:, LANES:]


def _with_ones(v_bf16):
    return jnp.concatenate([v_bf16, jnp.ones(v_bf16.shape, BF16)], axis=1)


def _new_token_mask(rows, keys, t_pad, n_new):
    r = lax.broadcasted_iota(jnp.int32, (rows, keys), 0) % t_pad
    c = lax.broadcasted_iota(jnp.int32, (rows, keys), 1)
    return (c <= r) & (c < n_new)


def _head_rows(ref, idx, h, n_heads, page):
    return ref[idx + (pl.ds(h, page, stride=n_heads), slice(None))]


def _topk_mask(sc, lane, n_sel):
    sel = jnp.zeros(sc.shape, F32)
    for _ in range(n_sel):
        mx = jnp.max(sc, axis=-1, keepdims=True)
        idx = jnp.min(jnp.where(sc == mx, lane, LANES), axis=-1, keepdims=True)
        pick = lane == idx
        sel = jnp.where(pick & (mx > 0.5 * NEG_INF), 1.0, sel)
        sc = jnp.where(pick, BELOW_NEG_INF, sc)
    return sel


def _moba_prompt_kernel(q_ref, k_ref, v_ref, o_ref, kmean_ref, kext_ref, vext_ref, m_ref, acc_ref,
                        *, blk, span_blocks, scale, n_sel):
    i = pl.program_id(1)
    tq = q_ref.shape[0]
    s_len, dh = k_ref.shape
    nb = s_len // blk
    span = span_blocks * blk

    @pl.when(i == 0)
    def _():
        kmean_ref[...] = jnp.zeros_like(kmean_ref)
        kmean_ref[0:nb, :] = jnp.mean(k_ref[...].reshape(nb, blk, dh), axis=1)
        blk_lane = lax.broadcasted_iota(jnp.int32, (blk, LANES), 1)
        for n in range(nb):
            rows = slice(n * blk, (n + 1) * blk)
            kext_ref[rows, 0:LANES] = k_ref[rows, :].astype(BF16)
            kext_ref[rows, LANES:] = jnp.where(blk_lane == n, 1.0, 0.0).astype(BF16)
            vext_ref[rows, :] = _with_ones(v_ref[rows, :].astype(BF16))

    q = q_ref[...]
    lane = lax.broadcasted_iota(jnp.int32, (tq, LANES), 1)
    own = (i * tq + lax.broadcasted_iota(jnp.int32, (tq, LANES), 0)) // blk
    sc = _dot_nt(q, kmean_ref[...], precision=lax.Precision.HIGHEST)
    sel = _topk_mask(jnp.where(lane < own, sc, NEG_INF), lane, n_sel)
    sel = jnp.where(lane == own, 1.0, sel)
    q_ext = jnp.concatenate([q.astype(BF16), ((1.0 - sel) * NEG_INF).astype(BF16)], axis=1)

    _flash_reset(m_ref, acc_ref)
    last = (i * tq) // span
    rows = pl.ds(pl.multiple_of(last * span, span), span)
    s = _dot_nt(q_ext, kext_ref[rows, :])
    q_pos = i * tq + lax.broadcasted_iota(jnp.int32, s.shape, 0)
    k_pos = last * span + lax.broadcasted_iota(jnp.int32, s.shape, 1)
    _flash_update(jnp.where(k_pos <= q_pos, s, NEG_INF), vext_ref[rows, :], m_ref, acc_ref, scale)

    _flash_over_tiles(last, lambda c: pl.ds(pl.multiple_of(c * span, span), span),
                      lambda rows: _dot_nt(q_ext, kext_ref[rows, :]), lambda rows: vext_ref[rows, :],
                      m_ref, acc_ref, scale)
    o_ref[...] = _flash_result(acc_ref).astype(BF16)


def _moba_prompt(h_all, cols, n_heads, dh):
    s = h_all.shape[0]
    blk = MOBA_BLOCK
    nb = s // blk
    span_blocks = math.gcd(nb, 4)
    tq = math.gcd(span_blocks, 2) * blk
    assert s % blk == 0 and nb <= LANES and dh == LANES
    cq, ck, cv = cols["mq"] // dh, cols["mk"] // dh, cols["mv"] // dh
    return pl.pallas_call(
        functools.partial(_moba_prompt_kernel, blk=blk, span_blocks=span_blocks, scale=dh ** -0.5,
                          n_sel=min(MOBA_TOPK, nb - 1)),
        out_shape=jax.ShapeDtypeStruct((s, n_heads * dh), BF16),
        grid=(n_heads, s // tq),
        in_specs=[pl.BlockSpec((tq, dh), lambda h, i: (i, cq + h)),
                  pl.BlockSpec((s, dh), lambda h, i: (0, ck + h)),
                  pl.BlockSpec((s, dh), lambda h, i: (0, cv + h))],
        out_specs=pl.BlockSpec((tq, dh), lambda h, i: (i, h)),
        scratch_shapes=[pltpu.VMEM((LANES, dh), F32), pltpu.VMEM((s, 2 * LANES), BF16),
                        pltpu.VMEM((s, 2 * LANES), BF16), pltpu.VMEM((tq, LANES), F32),
                        pltpu.VMEM((tq, 2 * LANES), F32)],
        compiler_params=_params("arbitrary", "arbitrary"),
        name="moba_prompt",
    )(h_all, h_all, h_all)


def _moba_sample_kernel(pt_ref, q_ref, *refs, pps, ppb, n_heads, t_pad, n_new, scale, n_sel):
    k_refs, v_refs = refs[:pps], refs[pps:2 * pps]
    kn_ref, vn_ref, o_ref, m_ref, l_ref, sc_ref, part_ref = refs[2 * pps:]
    step = pl.program_id(1)
    page = kn_ref.shape[1] // n_heads
    lane = lax.broadcasted_iota(jnp.int32, (t_pad, LANES), 1)

    @pl.when(step == 0)
    def _():
        m_ref[...] = jnp.full(m_ref.shape, NEG_INF, F32)
        l_ref[...] = jnp.zeros(l_ref.shape, F32)
        sc_ref[...] = jnp.full(sc_ref.shape, NEG_INF, F32)

    blocks = pps // ppb
    blk = ppb * page
    s_raw = [_dot_nt(q_ref[0, h], jnp.concatenate(
        [_head_rows(r, (0, 0), h, n_heads, page) for r in k_refs], 0).astype(BF16)) for h in range(n_heads)]
    m_new, l_new, sc_new = ([m_ref[h] for h in range(n_heads)], [l_ref[h] for h in range(n_heads)],
                            [sc_ref[h] for h in range(n_heads)])
    probs = []
    for h in range(n_heads):
        for j in range(blocks):
            s_blk = s_raw[h][:, j * blk:(j + 1) * blk]
            m_n = jnp.max(s_blk, axis=-1, keepdims=True) * scale
            probs.append(jnp.exp(s_blk * scale - m_n).astype(BF16))
            here = lane == step * blocks + j
            m_new[h] = jnp.where(here, m_n, m_new[h])
            sc_new[h] = jnp.where(here, jnp.mean(s_blk, axis=-1, keepdims=True), sc_new[h])
    parts = []
    for h in range(n_heads):
        for j in range(blocks):
            vb = jnp.concatenate([_head_rows(v_refs[g], (0, 0), h, n_heads, page)
                                  for g in range(j * ppb, (j + 1) * ppb)], 0).astype(BF16)
            part = _dot(probs[h * blocks + j], _with_ones(vb))
            parts.append(part[:, :LANES])
            l_new[h] = jnp.where(lane == step * blocks + j, part[:, LANES:], l_new[h])
    part_ref[step] = jnp.concatenate(parts, axis=0)
    for h in range(n_heads):
        m_ref[h], l_ref[h], sc_ref[h] = m_new[h], l_new[h], sc_new[h]

    @pl.when(step == pl.num_programs(1) - 1)
    def _():
        for h in range(n_heads):
            qb = q_ref[0, h]
            sel = _topk_mask(sc_ref[h], lane, n_sel) > 0.5
            s_own = _dot_nt(qb, _head_rows(kn_ref, (0,), h, n_heads, page).astype(BF16)) * scale
            s_own = jnp.where(_new_token_mask(t_pad, page, t_pad, n_new), s_own, NEG_INF)
            m_blocks = m_ref[h]
            m_all = jnp.maximum(jnp.max(jnp.where(sel, m_blocks, NEG_INF), axis=-1, keepdims=True),
                                jnp.max(s_own, axis=-1, keepdims=True))
            p_own = jnp.exp(s_own - m_all)
            w = jnp.where(sel, jnp.exp(m_blocks - m_all), 0.0)
            l_all = jnp.sum(w * l_ref[h], axis=-1, keepdims=True) + jnp.sum(p_own, axis=-1, keepdims=True)
            acc = _dot(p_own.astype(BF16), _head_rows(vn_ref, (0,), h, n_heads, page).astype(BF16))
            for b in range(part_ref.shape[0] * blocks):
                row = (h * blocks + b % blocks) * t_pad
                acc = acc + w[:, b:b + 1] * part_ref[b // blocks, row:row + t_pad, :]
            o_ref[0, :, h * LANES:(h + 1) * LANES] = (acc / l_all).astype(BF16)


def _moba_sample(layer, q, k_new, v_new, cache_k, cache_v, page_table, n_heads, n_new, pps):
    b, _, t_pad, dh = q.shape
    page = cache_k.shape[2] // n_heads
    n_pages = page_table.shape[1]
    ppb = MOBA_BLOCK // page
    nb = n_pages // ppb
    assert ppb * page == MOBA_BLOCK and n_pages % pps == 0 and pps % ppb == 0 and nb <= LANES and dh == LANES

    def page_spec(g):
        return pl.BlockSpec((1, 1, page * n_heads, dh), lambda i, n, pt: (layer, pt[i, n * pps + g], 0, 0))

    new_spec = pl.BlockSpec((1, page * n_heads, dh), lambda i, n, pt: (i, 0, 0))
    stat = pltpu.VMEM((n_heads, t_pad, LANES), F32)
    grid_spec = pltpu.PrefetchScalarGridSpec(
        num_scalar_prefetch=1,
        grid=(b, n_pages // pps),
        in_specs=[pl.BlockSpec((1, n_heads, t_pad, dh), lambda i, n, pt: (i, 0, 0, 0))]
                 + [page_spec(g) for g in range(pps)] * 2 + [new_spec, new_spec],
        out_specs=pl.BlockSpec((1, t_pad, n_heads * dh), lambda i, n, pt: (i, 0, 0)),
        scratch_shapes=[stat, stat, stat,
                        pltpu.VMEM((n_pages // pps, (pps // ppb) * n_heads * t_pad, dh), F32)],
    )
    return pl.pallas_call(
        functools.partial(_moba_sample_kernel, pps=pps, ppb=ppb, n_heads=n_heads, t_pad=t_pad, n_new=n_new,
                          scale=dh ** -0.5, n_sel=min(MOBA_TOPK, nb)),
        out_shape=jax.ShapeDtypeStruct((b, t_pad, n_heads * dh), BF16),
        grid_spec=grid_spec,
        compiler_params=_params("parallel", "arbitrary"),
        name="moba_sample",
    )(page_table, q, *([cache_k] * pps), *([cache_v] * pps), k_new, v_new)


def _diff_lambda(lqk, lam_init):
    s1 = jnp.sum(lqk[0:1] * lqk[1:2], axis=-1, keepdims=True)
    s2 = jnp.sum(lqk[2:3] * lqk[3:4], axis=-1, keepdims=True)
    return jnp.exp(s1) - jnp.exp(s2) + lam_init


def _diff_finish(o1, o2, lqk, g, lam_init):
    od = o1 - _diff_lambda(lqk, lam_init) * o2
    od = od * lax.rsqrt(jnp.mean(od * od, axis=-1, keepdims=True) + NORM_EPS) * g
    return od * (1.0 - lam_init)


def _split_sub_heads(q):
    first = lax.broadcasted_iota(jnp.int32, q.shape, q.ndim - 1) < LANES // 2
    return jnp.concatenate([jnp.where(first, q, 0.0), jnp.where(first, 0.0, q)], axis=q.ndim - 2)


def _diff_prompt_kernel(q_ref, k_ref, v_ref, g_ref, lqk_ref, o_ref, kb_ref, vext_ref, m_ref, acc_ref,
                        *, tile, scale, lam_init):
    i = pl.program_id(1)
    s_len = k_ref.shape[0]

    @pl.when(i == 0)
    def _():
        for c in range(s_len // tile):
            rows = slice(c * tile, (c + 1) * tile)
            kb_ref[rows, :] = k_ref[rows, :].astype(BF16)
            vext_ref[rows, :] = _with_ones(v_ref[rows, :].astype(BF16))

    fold = _is_pow2(scale)
    qs = _split_sub_heads(q_ref[...])
    qs = (qs * scale if fold else qs).astype(BF16)

    def scores(rows):
        s = _dot_nt(qs, kb_ref[rows, :])
        return s if fold else s * scale

    _flash_reset(m_ref, acc_ref)
    diag = pl.ds(pl.multiple_of(i * tile, tile), tile)
    s = scores(diag)
    row = lax.broadcasted_iota(jnp.int32, s.shape, 0) % tile
    col = lax.broadcasted_iota(jnp.int32, s.shape, 1)
    _flash_update(jnp.where(col <= row, s, NEG_INF), vext_ref[diag, :], m_ref, acc_ref)

    _flash_over_tiles(i, lambda n: pl.ds(pl.multiple_of(n * tile, tile), tile), scores,
                      lambda rows: vext_ref[rows, :], m_ref, acc_ref)
    o = _flash_result(acc_ref)
    o_ref[...] = _diff_finish(o[:tile], o[tile:], lqk_ref[...], g_ref[...], lam_init).astype(BF16)


def _diff_prompt(h_all, cols, lqk, subln_g, lam_init, n_heads, dh, tile):
    s = h_all.shape[0]
    w = 2 * dh
    assert w == LANES and s % tile == 0
    cq, ck, cv = cols["dq"] // w, cols["dk"] // w, cols["dv"] // w
    return pl.pallas_call(
        functools.partial(_diff_prompt_kernel, tile=tile, scale=dh ** -0.5, lam_init=lam_init),
        out_shape=jax.ShapeDtypeStruct((s, n_heads * w), BF16),
        grid=(n_heads, s // tile),
        in_specs=[pl.BlockSpec((tile, w), lambda h, i: (i, cq + h)),
                  pl.BlockSpec((s, w), lambda h, i: (0, ck + h)),
                  pl.BlockSpec((s, w), lambda h, i: (0, cv + h)),
                  pl.BlockSpec((1, w), lambda h, i: (0, 0)),
                  pl.BlockSpec(lqk.shape, lambda h, i: (0, 0))],
        out_specs=pl.BlockSpec((tile, w), lambda h, i: (i, h)),
        scratch_shapes=[pltpu.VMEM((s, w), BF16), pltpu.VMEM((s, 2 * LANES), BF16),
                        pltpu.VMEM((2 * tile, LANES), F32), pltpu.VMEM((2 * tile, 2 * LANES), F32)],
        compiler_params=_params("arbitrary", "arbitrary"),
        name="diff_prompt",
    )(h_all, h_all, h_all, subln_g.reshape(1, w), lqk)


def _diff_sample_kernel(pt_ref, q_ref, *refs, pps, n_heads, t_pad, n_new, lam_init):
    k_refs, v_refs = refs[:pps], refs[pps:2 * pps]
    kn_ref, vn_ref, g_ref, lqk_ref, o_ref, m_ref, acc_ref = refs[2 * pps:]
    step = pl.program_id(1)
    page = kn_ref.shape[2]

    def head_dims(h):
        return slice(h * LANES, (h + 1) * LANES)

    @pl.when(step == 0)
    def _():
        _flash_reset(m_ref, acc_ref)
        for h in range(n_heads):
            s = _dot(q_ref[0, h], kn_ref[0, head_dims(h), :].astype(BF16))
            s = jnp.where(_new_token_mask(2 * t_pad, page, t_pad, n_new), s, NEG_INF)
            _flash_update(s, _with_ones(_head_rows(vn_ref, (0,), h, n_heads, page).astype(BF16)),
                          m_ref.at[h], acc_ref.at[h])

    scores = [_dot(q_ref[0, h], jnp.concatenate([r[0, 0, head_dims(h), :] for r in k_refs], 1).astype(BF16))
              for h in range(n_heads)]
    weights = [_flash_weights(scores[h], m_ref.at[h]) for h in range(n_heads)]
    for h in range(n_heads):
        vb = jnp.concatenate([_head_rows(r, (0, 0), h, n_heads, page) for r in v_refs], 0).astype(BF16)
        _flash_accumulate(*weights[h], _with_ones(vb), acc_ref.at[h])

    @pl.when(step == pl.num_programs(1) - 1)
    def _():
        for h in range(n_heads):
            o = _flash_result(acc_ref.at[h])
            o_ref[0, :, h * LANES:(h + 1) * LANES] = _diff_finish(
                o[:t_pad], o[t_pad:], lqk_ref[...], g_ref[...], lam_init).astype(BF16)


def _diff_sample(layer, q, k_new, v_new, cache_k, cache_v, page_table, lqk, subln_g, lam_init,
                 n_heads, n_new, pps):
    b, _, rows, w = q.shape
    t_pad = rows // 2
    page = cache_k.shape[3]
    n_pages = page_table.shape[1]
    assert n_pages % pps == 0 and w == LANES and page == LANES

    def page_spec(g, shape):
        return pl.BlockSpec((1, 1) + shape, lambda i, n, pt: (layer, pt[i, n * pps + g], 0, 0))

    def new_spec(shape):
        return pl.BlockSpec((1,) + shape, lambda i, n, pt: (i, 0, 0))

    k_shape, v_shape = (n_heads * w, page), (page * n_heads, w)
    grid_spec = pltpu.PrefetchScalarGridSpec(
        num_scalar_prefetch=1,
        grid=(b, n_pages // pps),
        in_specs=[pl.BlockSpec((1, n_heads, rows, w), lambda i, n, pt: (i, 0, 0, 0))]
                 + [page_spec(g, k_shape) for g in range(pps)] + [page_spec(g, v_shape) for g in range(pps)]
                 + [new_spec(k_shape), new_spec(v_shape),
                    pl.BlockSpec((1, LANES), lambda i, n, pt: (0, 0)),
                    pl.BlockSpec(lqk.shape, lambda i, n, pt: (0, 0))],
        out_specs=pl.BlockSpec((1, t_pad, n_heads * w), lambda i, n, pt: (i, 0, 0)),
        scratch_shapes=[pltpu.VMEM((n_heads, rows, LANES), F32), pltpu.VMEM((n_heads, rows, 2 * LANES), F32)],
    )
    return pl.pallas_call(
        functools.partial(_diff_sample_kernel, pps=pps, n_heads=n_heads, t_pad=t_pad, n_new=n_new,
                          lam_init=lam_init),
        out_shape=jax.ShapeDtypeStruct((b, t_pad, n_heads * w), BF16),
        grid_spec=grid_spec,
        compiler_params=_params("parallel", "arbitrary"),
        name="diff_sample",
    )(page_table, q, *([cache_k] * pps), *([cache_v] * pps), k_new, v_new, subln_g.reshape(1, LANES), lqk)


def _merge_kernel(x_ref, yr_ref, om_ref, od_ref, gates_ref, wr_ref, wm_ref, wd_ref, wo_ref, o_ref):
    d = x_ref.shape[1]
    mix = (jax.nn.sigmoid(gates_ref[:, 0:d]) * _dot(yr_ref[...], wr_ref[...])
           + jax.nn.sigmoid(gates_ref[:, d:2 * d]) * _dot(om_ref[...], wm_ref[...])
           + jax.nn.sigmoid(gates_ref[:, 2 * d:3 * d]) * _dot(od_ref[...], wd_ref[...]))
    o_ref[...] = x_ref[...] + _dot(mix.astype(BF16), wo_ref[...])


def _merge(x, y_ret, o_moba, o_diff, h_all, gate_col, w_ret, w_moba, w_diff, w_out, layer, tm):
    m, d = x.shape
    assert gate_col % (3 * d) == 0

    def rows(w):
        return pl.BlockSpec((tm, w), lambda i: (i, 0))

    def whole(arr):
        return pl.BlockSpec((None,) + arr.shape[1:], lambda i: (layer, 0, 0), pipeline_mode=pl.Buffered(1))

    return pl.pallas_call(
        _merge_kernel,
        out_shape=jax.ShapeDtypeStruct((m, d), F32),
        grid=(m // tm,),
        in_specs=[rows(d), rows(y_ret.shape[1]), rows(o_moba.shape[1]), rows(o_diff.shape[1]),
                  pl.BlockSpec((tm, 3 * d), lambda i: (i, gate_col // (3 * d))),
                  whole(w_ret), whole(w_moba), whole(w_diff), whole(w_out)],
        out_specs=rows(d),
        compiler_params=_params("parallel"),
        name="merge_out",
    )(x, y_ret, o_moba, o_diff, h_all, w_ret, w_moba, w_diff, w_out)


def _ffn_kernel(x_ref, g_ref, w1_ref, w2_ref, fg_ref, o_ref, xn_ref, *, final_norm):
    t = pl.program_id(1)

    @pl.when(t == 0)
    def _():
        x = x_ref[...]
        xn_ref[...] = _rms(x, g_ref[...]).astype(BF16)
        o_ref[...] = x

    hid = jnp.square(jnp.maximum(_dot(xn_ref[...], w1_ref[...]), 0.0))
    o_ref[...] += _dot(hid.astype(BF16), w2_ref[...])

    if final_norm:
        @pl.when(t == pl.num_programs(1) - 1)
        def _():
            o_ref[...] = _rms(o_ref[...], fg_ref[...])


def _ffn(x, g, w1, w2, layer, final_g, final_norm, tm, tf):
    m, d = x.shape
    f = w1.shape[2]
    return pl.pallas_call(
        functools.partial(_ffn_kernel, final_norm=final_norm),
        out_shape=jax.ShapeDtypeStruct((m, d), F32),
        grid=(m // tm, f // tf),
        in_specs=[pl.BlockSpec((tm, d), lambda i, t: (i, 0)),
                  pl.BlockSpec((1, d), lambda i, t: (0, 0)),
                  pl.BlockSpec((None, d, tf), lambda i, t: (layer, 0, t)),
                  pl.BlockSpec((None, tf, d), lambda i, t: (layer, t, 0)),
                  pl.BlockSpec((1, d), lambda i, t: (0, 0))],
        out_specs=pl.BlockSpec((tm, d), lambda i, t: (i, 0)),
        scratch_shapes=[pltpu.VMEM((tm, d), BF16)],
        compiler_params=_params("parallel", "arbitrary"),
        name="ffn",
    )(x, g.reshape(1, d), w1, w2, final_g.reshape(1, d))


def _pad_rows(x, rows):
    return jnp.pad(x, ((0, 0), (0, rows - x.shape[1]), (0, 0)))


def _per_head(x, n_heads, t_pad):
    b = x.shape[0]
    return _pad_rows(x, t_pad).reshape(b, t_pad, n_heads, LANES).transpose(0, 2, 1, 3)


def _as_page(x, n_heads, page):
    return _pad_rows(x, page).reshape(x.shape[0], page * n_heads, LANES)


def kernel(x_prompt, x_sample, cache_moba_k, cache_moba_v, cache_diff_k, cache_diff_v, state_ret, page_table, norm1_g, w_in, diff_lq1, diff_lk1, diff_lq2, diff_lk2, diff_subln_g, w_up_ret, w_up_moba, w_up_diff, w_out, norm2_g, w_ff1, w_ff2, final_g):
    bp, s_len, d = x_prompt.shape
    bs, t_new, _ = x_sample.shape
    depth, n_pool, page, moba_h, moba_dh = cache_moba_k.shape
    diff_h, diff_dh = cache_diff_k.shape[3], cache_diff_k.shape[5]
    ret_h, ret_dk, ret_dv = state_ret.shape[2:]
    n_pages = page_table.shape[1]
    past_len = n_pages * page
    assert bp == 1 and ret_dk == LANES and moba_dh == LANES and 2 * diff_dh == LANES
    assert t_new <= SUBLANES and past_len % MOBA_BLOCK == 0

    ret_qk_w, ret_v_w = ret_h * ret_dk, ret_h * ret_dv
    moba_w, diff_w = moba_h * moba_dh, diff_h * 2 * diff_dh
    names = ["rq", "rk", "rv", "rg", "mq", "mk", "mv", "dq", "dk", "dv", "gr", "gm", "gd"]
    widths = [ret_qk_w, ret_qk_w, ret_v_w, ret_v_w, moba_w, moba_w, moba_w, diff_w, diff_w, diff_w, d, d, d]
    cols, off = {}, 0
    for nm, wd in zip(names, widths):
        cols[nm] = off
        off += wd
    in_w = off
    assert w_in.shape == (depth, d, in_w)

    tn = 512
    kinds = {"rq": EPI_ROPE128, "rk": EPI_ROPE128_SCALED, "mq": EPI_ROPE128, "mk": EPI_ROPE128,
             "dq": EPI_ROPE64, "dk": EPI_ROPE64}
    tile_kinds = []
    for nm, wd in zip(names, widths):
        assert wd % tn == 0
        tile_kinds += [kinds.get(nm, EPI_NONE)] * (wd // tn)
    tile_kinds = tuple(tile_kinds)

    m_s = bs * t_new
    xp = x_prompt.reshape(s_len, d)
    xs = x_sample.reshape(m_s, d)
    tables_p = _rope_tables(np.arange(s_len))
    tables_s = _rope_tables(np.tile(past_len + np.arange(t_new), bs))
    tm_p = min(1024, s_len)
    tm_ffn = min(512, s_len)
    tm_merge = min(256, s_len)
    tf = min(512, w_ff1.shape[-1])
    diff_tile = min(512, s_len)
    ret_chunk = min(256, s_len)
    t_pad = SUBLANES
    pps = min(8, n_pages)

    ck_moba = cache_moba_k.reshape(depth, n_pool, page * moba_h, moba_dh)
    cv_moba = cache_moba_v.reshape(depth, n_pool, page * moba_h, moba_dh)
    ck_diff = cache_diff_k.transpose(0, 1, 3, 4, 5, 2).reshape(depth, n_pool, diff_w, page)
    cv_diff = cache_diff_v.reshape(depth, n_pool, page * diff_h, 2 * diff_dh)

    w_in_b, w_ret_b, w_moba_b, w_diff_b, w_out_b, w_ff1_b, w_ff2_b = (
        w.astype(BF16) for w in (w_in, w_up_ret, w_up_moba, w_up_diff, w_out, w_ff1, w_ff2))

    outs = {k: [] for k in ["pmk", "pmv", "pdk", "pdv", "pst", "smk", "smv", "sdk", "sdv", "sst"]}
    for l in range(depth):
        lam_init = 0.8 - 0.6 * math.exp(-0.3 * l)
        lqk = jnp.stack([diff_lq1[l], diff_lk1[l], diff_lq2[l], diff_lk2[l]])
        last = l == depth - 1
        k_scale = ret_dk ** -0.5

        def sec(h_all, nm, wd):
            return h_all[:, cols[nm]:cols[nm] + wd]

        hp = _in_proj(xp, norm1_g[l], w_in_b, l, tables_p, tile_kinds, k_scale, tm_p, tn)
        y_ret, s_fin = _ret_prompt(hp, cols, ret_h, ret_dk, ret_dv, ret_chunk)
        o_moba = _moba_prompt(hp, cols, moba_h, moba_dh)
        o_diff = _diff_prompt(hp, cols, lqk, diff_subln_g[l], lam_init, diff_h, diff_dh, diff_tile)
        x1 = _merge(xp, y_ret, o_moba, o_diff, hp, cols["gr"], w_ret_b, w_moba_b, w_diff_b, w_out_b, l, tm_merge)
        xp = _ffn(x1, norm2_g[l], w_ff1_b, w_ff2_b, l, final_g, last, tm_ffn, tf)
        outs["pmk"].append(sec(hp, "mk", moba_w).reshape(bp, s_len, moba_h, moba_dh))
        outs["pmv"].append(sec(hp, "mv", moba_w).reshape(bp, s_len, moba_h, moba_dh))
        outs["pdk"].append(sec(hp, "dk", diff_w).reshape(bp, s_len, diff_h, 2, diff_dh))
        outs["pdv"].append(sec(hp, "dv", diff_w).reshape(bp, s_len, diff_h, 2 * diff_dh))
        outs["pst"].append(s_fin.reshape(bp, ret_h, ret_dk, ret_dv))

        hs = _in_proj(xs, norm1_g[l], w_in_b, l, tables_s, tile_kinds, k_scale, m_s, tn)

        def sec3(nm, wd):
            return sec(hs, nm, wd).reshape(bs, t_new, wd)

        y_ret_s, s_new = _ret_sample(
            _pad_rows(sec3("rq", ret_qk_w), t_pad), _pad_rows(sec3("rk", ret_qk_w), t_pad),
            _pad_rows(sec3("rv", ret_v_w), t_pad), _pad_rows(sec3("rg", ret_v_w), t_pad),
            state_ret[l], ret_h, ret_dk, ret_dv, t_new)
        mk_s, mv_s = sec3("mk", moba_w), sec3("mv", moba_w)
        o_moba_s = _moba_sample(l, _per_head(sec3("mq", moba_w), moba_h, t_pad).astype(BF16),
                                _as_page(mk_s, moba_h, page), _as_page(mv_s, moba_h, page),
                                ck_moba, cv_moba, page_table, moba_h, t_new, pps)
        dk_s, dv_s = sec3("dk", diff_w), sec3("dv", diff_w)
        diff_scale = diff_dh ** -0.5
        assert _is_pow2(diff_scale)
        dq_s = _split_sub_heads(_per_head(sec3("dq", diff_w) * diff_scale, diff_h, t_pad)).astype(BF16)
        o_diff_s = _diff_sample(l, dq_s, _pad_rows(dk_s, page).transpose(0, 2, 1), _as_page(dv_s, diff_h, page),
                                ck_diff, cv_diff, page_table, lqk, diff_subln_g[l], lam_init,
                                diff_h, t_new, pps)

        def unpad(o):
            return o[:, :t_new].reshape(m_s, o.shape[-1])

        x1s = _merge(xs, unpad(y_ret_s), unpad(o_moba_s), unpad(o_diff_s), hs, cols["gr"],
                     w_ret_b, w_moba_b, w_diff_b, w_out_b, l, m_s)
        xs = _ffn(x1s, norm2_g[l], w_ff1_b, w_ff2_b, l, final_g, last, m_s, tf)
        outs["smk"].append(mk_s.reshape(bs, t_new, moba_h, moba_dh))
        outs["smv"].append(mv_s.reshape(bs, t_new, moba_h, moba_dh))
        outs["sdk"].append(dk_s.reshape(bs, t_new, diff_h, 2, diff_dh))
        outs["sdv"].append(dv_s.reshape(bs, t_new, diff_h, 2 * diff_dh))
        outs["sst"].append(s_new)

    st = {k: jnp.stack(v) for k, v in outs.items()}
    return (xp.reshape(bp, s_len, d), xs.reshape(bs, t_new, d),
            st["pmk"], st["pmv"], st["pdk"], st["pdv"], st["pst"],
            st["smk"], st["smv"], st["sdk"], st["sdv"], st["sst"])
```

```python
import functools
import math

import numpy as np
import jax
import jax.numpy as jnp
from jax import lax
from jax.experimental import pallas as pl
from jax.experimental.pallas import tpu as pltpu

NORM_EPS = 1e-6
NEG_INF = -1e30
BELOW_NEG_INF = -3e38
ROPE_THETA = 10000.0
MOBA_BLOCK = 256
MOBA_TOPK = 3
LANES = 128
SUBLANES = 8
VMEM_LIMIT_BYTES = 48 * 1024 * 1024
LOG2_E = 1.4426950408889634

BF16 = jnp.bfloat16
F32 = jnp.float32


def _dot(a, b):
    return jnp.dot(a, b, preferred_element_type=F32)


def _dot_nt(a, b, precision=None):
    return lax.dot_general(a, b, (((1,), (1,)), ((), ())), precision=precision,
                           preferred_element_type=F32)


def _dot_tn(a, b):
    return lax.dot_general(a, b, (((0,), (0,)), ((), ())), preferred_element_type=F32)


def _params(*sem):
    return pltpu.CompilerParams(dimension_semantics=sem, vmem_limit_bytes=VMEM_LIMIT_BYTES)


def _rms(x, g):
    return x * lax.rsqrt(jnp.mean(x * x, axis=-1, keepdims=True) + NORM_EPS) * g


def _is_pow2(v):
    m, _ = math.frexp(v)
    return m == 0.5


EPI_NONE, EPI_ROPE128, EPI_ROPE128_SCALED, EPI_ROPE64 = 0, 1, 2, 3


def _in_proj_kernel(x_ref, g_ref, w_ref, ca_ref, sa_ref, cb_ref, sb1_ref, sb2_ref, o_ref, *rest,
                    tile_kinds, copy_tiles, k_scale):
    copy_refs, xn_ref = rest[:-1], rest[-1]
    j = pl.program_id(1)
    tm = o_ref.shape[0]

    @pl.when(j == 0)
    def _():
        xn_ref[...] = _rms(x_ref[...], g_ref[...]).astype(BF16)

    for rows in ([slice(0, tm // 2), slice(tm // 2, tm)] if tm % (2 * SUBLANES) == 0 else [slice(0, tm)]):
        o_ref[rows, :] = _dot(xn_ref[rows, :], w_ref[...])
    n_groups = o_ref.shape[1] // LANES

    def among(kind):
        js = [t for t, kd in enumerate(tile_kinds) if kd == kind]
        cond = j == js[0]
        for t in js[1:]:
            cond = cond | (j == t)
        return cond

    def rope128(scale):
        ca, sa = ca_ref[...], sa_ref[...]
        for c in range(n_groups):
            seg = o_ref[:, c * LANES:(c + 1) * LANES]
            out = seg * ca + pltpu.roll(seg, LANES // 2, axis=1) * sa
            o_ref[:, c * LANES:(c + 1) * LANES] = out if scale is None else out * scale

    @pl.when(among(EPI_ROPE128))
    def _():
        rope128(None)

    @pl.when(among(EPI_ROPE128_SCALED))
    def _():
        rope128(k_scale)

    @pl.when(among(EPI_ROPE64))
    def _():
        cb, sb1, sb2 = cb_ref[...], sb1_ref[...], sb2_ref[...]
        for c in range(n_groups):
            seg = o_ref[:, c * LANES:(c + 1) * LANES]
            up = pltpu.roll(seg, LANES - LANES // 4, axis=1)
            down = pltpu.roll(seg, LANES // 4, axis=1)
            o_ref[:, c * LANES:(c + 1) * LANES] = seg * cb + up * sb1 + down * sb2

    for copy_ref, tile in zip(copy_refs, copy_tiles):
        @pl.when(j == tile)
        def _(copy_ref=copy_ref):
            for c in range(n_groups):
                copy_ref[pl.ds(c, tm, stride=n_groups), :] = o_ref[:, c * LANES:(c + 1) * LANES]


def _rope_tables(pos):
    f32 = np.float32
    pos = np.asarray(pos, np.float64)[:, None]
    half_a, half_b = LANES // 2, LANES // 4
    inv_a = ROPE_THETA ** (-np.arange(half_a, dtype=np.float64) / half_a)
    inv_b = ROPE_THETA ** (-np.arange(half_b, dtype=np.float64) / half_b)
    ang_a = pos * inv_a[None, :]
    ang_b = pos * inv_b[None, :]
    cos_a, sin_a = np.cos(ang_a), np.sin(ang_a)
    cos_b, sin_b = np.cos(ang_b), np.sin(ang_b)
    zero_b = np.zeros_like(sin_b)
    ca = np.concatenate([cos_a, cos_a], -1)
    sa = np.concatenate([-sin_a, sin_a], -1)
    cb = np.concatenate([cos_b] * 4, -1)
    sb1 = np.concatenate([-sin_b, zero_b] * 2, -1)
    sb2 = np.concatenate([zero_b, sin_b] * 2, -1)
    return tuple(jnp.asarray(t.astype(f32)) for t in (ca, sa, cb, sb1, sb2))


def _in_proj(x, g, w_all, layer, tables, tile_kinds, copy_tiles, k_scale, tm, tn):
    m, d = x.shape
    n = w_all.shape[2]
    groups = tn // LANES
    tab_spec = pl.BlockSpec((tm, LANES), lambda i, j: (i, 0))
    copy_shape = jax.ShapeDtypeStruct((m * groups, LANES), F32)
    copy_spec = pl.BlockSpec((tm * groups, LANES), lambda i, j: (i, 0))
    return pl.pallas_call(
        functools.partial(_in_proj_kernel, tile_kinds=tile_kinds, copy_tiles=copy_tiles, k_scale=k_scale),
        out_shape=(jax.ShapeDtypeStruct((m, n), F32),) + (copy_shape,) * len(copy_tiles),
        grid=(m // tm, n // tn),
        in_specs=[pl.BlockSpec((tm, d), lambda i, j: (i, 0)),
                  pl.BlockSpec((1, d), lambda i, j: (0, 0)),
                  pl.BlockSpec((None, d, tn), lambda i, j: (layer, 0, j)),
                  tab_spec, tab_spec, tab_spec, tab_spec, tab_spec],
        out_specs=(pl.BlockSpec((tm, tn), lambda i, j: (i, j)),) + (copy_spec,) * len(copy_tiles),
        scratch_shapes=[pltpu.VMEM((tm, d), BF16)],
        compiler_params=_params("parallel", "arbitrary"),
        name="in_proj",
    )(x, g.reshape(1, d), w_all, *tables)


def _ret_log_g(h):
    return math.log(1.0 - 2.0 ** (-5.0 - h))


def _ret_heads(q_all, k_all, v_all, rg_all, get_state, set_state, write_y, *, n_heads, dk, dv, chunk):
    rows = q_all.shape[0]
    ri = lax.broadcasted_iota(jnp.int32, (rows, rows), 0)
    ci = lax.broadcasted_iota(jnp.int32, (rows, rows), 1)
    diff = (ri - ci).astype(F32)
    pos = lax.broadcasted_iota(jnp.int32, (rows, 1), 0).astype(F32)
    for h in range(n_heads):
        lg = _ret_log_g(h)
        q = q_all[:, h * dk:(h + 1) * dk]
        k = k_all[:, h * dk:(h + 1) * dk]
        vb = v_all[:, h * dv:(h + 1) * dv].astype(BF16)
        qb = q.astype(BF16)
        dmask = jnp.where(diff >= 0, jnp.exp(jnp.maximum(diff, 0.0) * lg), 0.0)
        a = _dot_nt(qb, k.astype(BF16)) * dmask
        state = get_state(h)
        o = _dot(a.astype(BF16), vb) + _dot(qb, state.astype(BF16)) * jnp.exp((pos + 1.0) * lg)
        kd = (k * jnp.exp((chunk - 1.0 - pos) * lg)).astype(BF16)
        set_state(h, state * math.exp(chunk * lg) + _dot_tn(kd, vb))
        yn = o * lax.rsqrt(jnp.mean(o * o, axis=-1, keepdims=True) + NORM_EPS)
        rg = rg_all[:, h * dv:(h + 1) * dv]
        write_y(h, (yn * (rg * jax.nn.sigmoid(rg))).astype(BF16))


def _ret_prompt_kernel(q_ref, k_ref, v_ref, rg_ref, y_ref, s_ref, *, n_heads, dk, dv, chunk):
    @pl.when(pl.program_id(0) == 0)
    def _():
        s_ref[...] = jnp.zeros_like(s_ref)

    def set_state(h, val):
        s_ref[h] = val

    def write_y(h, val):
        y_ref[:, h * dv:(h + 1) * dv] = val

    _ret_heads(q_ref[...], k_ref[...], v_ref[...], rg_ref[...], lambda h: s_ref[h], set_state, write_y,
               n_heads=n_heads, dk=dk, dv=dv, chunk=chunk)


def _ret_prompt(h_all, cols, n_heads, dk, dv, chunk):
    s = h_all.shape[0]
    qk_w, v_w = n_heads * dk, n_heads * dv
    return pl.pallas_call(
        functools.partial(_ret_prompt_kernel, n_heads=n_heads, dk=dk, dv=dv, chunk=chunk),
        out_shape=(jax.ShapeDtypeStruct((s, v_w), BF16),
                   jax.ShapeDtypeStruct((n_heads, dk, dv), F32)),
        grid=(s // chunk,),
        in_specs=[pl.BlockSpec((chunk, qk_w), lambda c: (c, cols["rq"] // qk_w)),
                  pl.BlockSpec((chunk, qk_w), lambda c: (c, cols["rk"] // qk_w)),
                  pl.BlockSpec((chunk, v_w), lambda c: (c, cols["rv"] // v_w)),
                  pl.BlockSpec((chunk, v_w), lambda c: (c, cols["rg"] // v_w))],
        out_specs=(pl.BlockSpec((chunk, v_w), lambda c: (c, 0)),
                   pl.BlockSpec((n_heads, dk, dv), lambda c: (0, 0, 0))),
        compiler_params=_params("arbitrary"),
        name="ret_prompt",
    )(h_all, h_all, h_all, h_all)


def _ret_sample_kernel(q_ref, k_ref, v_ref, rg_ref, s0_ref, y_ref, s_ref, *, n_heads, dk, dv, chunk):
    def set_state(h, val):
        s_ref[0, h] = val

    def write_y(h, val):
        y_ref[0, :, h * dv:(h + 1) * dv] = val

    _ret_heads(q_ref[0], k_ref[0], v_ref[0], rg_ref[0], lambda h: s0_ref[0, h], set_state, write_y,
               n_heads=n_heads, dk=dk, dv=dv, chunk=chunk)


def _ret_sample(q, k, v, rg, state, n_heads, dk, dv, chunk):
    b, rows, _ = q.shape
    qk_w, v_w = n_heads * dk, n_heads * dv

    def row_spec(w):
        return pl.BlockSpec((1, rows, w), lambda i: (i, 0, 0))

    st_spec = pl.BlockSpec((1, n_heads, dk, dv), lambda i: (i, 0, 0, 0))
    return pl.pallas_call(
        functools.partial(_ret_sample_kernel, n_heads=n_heads, dk=dk, dv=dv, chunk=chunk),
        out_shape=(jax.ShapeDtypeStruct((b, rows, v_w), BF16),
                   jax.ShapeDtypeStruct(state.shape, F32)),
        grid=(b,),
        in_specs=[row_spec(qk_w), row_spec(qk_w), row_spec(v_w), row_spec(v_w), st_spec],
        out_specs=(row_spec(v_w), st_spec),
        compiler_params=_params("parallel"),
        name="ret_sample",
    )(q, k, v, rg, state)


def _lane_tile(x, n):
    return x if n == 1 else jnp.concatenate([x] * n, axis=1)


def _flash_update(s, v_ext, m_ref, acc_ref, scale=None):
    p, alpha = _flash_weights(s, m_ref, scale)
    _flash_accumulate(p, alpha, v_ext, acc_ref)


def _flash_weights(s, m_ref, scale=None):
    m_old = m_ref[...]
    m_new = jnp.maximum(m_old, jnp.max(s, axis=-1, keepdims=True))
    m_wide = _lane_tile(m_new, s.shape[1] // LANES)
    if scale is None:
        alpha = jnp.exp(m_old - m_new)
        p = jnp.exp(s - m_wide)
    else:
        alpha = jnp.exp2((m_old - m_new) * (scale * LOG2_E))
        p = jnp.exp2((s - m_wide) * (scale * LOG2_E))
    m_ref[...] = m_new
    return p.astype(BF16), alpha


def _flash_accumulate(p, alpha, v_ext, acc_ref):
    acc_ref[...] = _lane_tile(alpha, 2) * acc_ref[...] + _dot(p, v_ext)


def _flash_over_tiles(n_tiles, tile_rows, score_fn, value_fn, m_ref, acc_ref, scale=None):
    def pair(c, carry):
        rows_a, rows_b = tile_rows(2 * c), tile_rows(2 * c + 1)
        s_a, s_b = score_fn(rows_a), score_fn(rows_b)
        _flash_update(s_a, value_fn(rows_a), m_ref, acc_ref, scale)
        _flash_update(s_b, value_fn(rows_b), m_ref, acc_ref, scale)
        return carry

    lax.fori_loop(0, n_tiles // 2, pair, 0)

    @pl.when(n_tiles % 2 == 1)
    def _():
        rows = tile_rows(n_tiles - 1)
        _flash_update(score_fn(rows), value_fn(rows), m_ref, acc_ref, scale)


def _flash_reset(m_ref, acc_ref):
    m_ref[...] = jnp.full(m_ref.shape, NEG_INF, F32)
    acc_ref[...] = jnp.zeros(acc_ref.shape, F32)


def _flash_result(acc_ref):
    acc = acc_ref[...]
    return acc[:, :LANES] / acc[:, LANES:]


def _with_ones(v_bf16):
    return jnp.concatenate([v_bf16, jnp.ones(v_bf16.shape, BF16)], axis=1)


def _new_token_mask(rows, keys, t_pad, n_new):
    r = lax.broadcasted_iota(jnp.int32, (rows, keys), 0) % t_pad
    c = lax.broadcasted_iota(jnp.int32, (rows, keys), 1)
    return (c <= r) & (c < n_new)


def _head_rows(ref, idx, h, n_heads, page):
    return ref[idx + (pl.ds(h, page, stride=n_heads), slice(None))]


def _topk_mask(sc, lane, n_sel):
    sel = jnp.zeros(sc.shape, F32)
    for _ in range(n_sel):
        mx = jnp.max(sc, axis=-1, keepdims=True)
        idx = jnp.min(jnp.where(sc == mx, lane, LANES), axis=-1, keepdims=True)
        pick = lane == idx
        sel = jnp.where(pick & (mx > 0.5 * NEG_INF), 1.0, sel)
        sc = jnp.where(pick, BELOW_NEG_INF, sc)
    return sel


def _moba_prompt_kernel(q_ref, k_ref, v_ref, o_ref, kmean_ref, kext_ref, vext_ref, m_ref, acc_ref,
                        *, blk, span_blocks, scale, n_sel):
    i = pl.program_id(1)
    tq = q_ref.shape[0]
    s_len, dh = k_ref.shape
    nb = s_len // blk
    span = span_blocks * blk

    @pl.when(i == 0)
    def _():
        kmean_ref[...] = jnp.zeros_like(kmean_ref)
        kmean_ref[0:nb, :] = jnp.mean(k_ref[...].reshape(nb, blk, dh), axis=1)
        blk_lane = lax.broadcasted_iota(jnp.int32, (blk, LANES), 1)
        for n in range(nb):
            rows = slice(n * blk, (n + 1) * blk)
            kext_ref[rows, 0:LANES] = k_ref[rows, :].astype(BF16)
            kext_ref[rows, LANES:] = jnp.where(blk_lane == n, 1.0, 0.0).astype(BF16)
            vext_ref[rows, :] = _with_ones(v_ref[rows, :].astype(BF16))

    q = q_ref[...]
    lane = lax.broadcasted_iota(jnp.int32, (tq, LANES), 1)
    own = (i * tq + lax.broadcasted_iota(jnp.int32, (tq, LANES), 0)) // blk
    sc = _dot_nt(q, kmean_ref[...], precision=lax.Precision.HIGHEST)
    sel = _topk_mask(jnp.where(lane < own, sc, NEG_INF), lane, n_sel)
    sel = jnp.where(lane == own, 1.0, sel)
    q_ext = jnp.concatenate([q.astype(BF16), ((1.0 - sel) * NEG_INF).astype(BF16)], axis=1)

    _flash_reset(m_ref, acc_ref)
    last = (i * tq) // span
    rows = pl.ds(pl.multiple_of(last * span, span), span)
    s = _dot_nt(q_ext, kext_ref[rows, :])
    q_pos = i * tq + lax.broadcasted_iota(jnp.int32, s.shape, 0)
    k_pos = last * span + lax.broadcasted_iota(jnp.int32, s.shape, 1)
    _flash_update(jnp.where(k_pos <= q_pos, s, NEG_INF), vext_ref[rows, :], m_ref, acc_ref, scale)

    _flash_over_tiles(last, lambda c: pl.ds(pl.multiple_of(c * span, span), span),
                      lambda rows: _dot_nt(q_ext, kext_ref[rows, :]), lambda rows: vext_ref[rows, :],
                      m_ref, acc_ref, scale)
    o_ref[...] = _flash_result(acc_ref).astype(BF16)


def _moba_prompt(h_all, cols, n_heads, dh):
    s = h_all.shape[0]
    blk = MOBA_BLOCK
    nb = s // blk
    span_blocks = math.gcd(nb, 4)
    tq = math.gcd(span_blocks, 2) * blk
    assert s % blk == 0 and nb <= LANES and dh == LANES
    cq, ck, cv = cols["mq"] // dh, cols["mk"] // dh, cols["mv"] // dh
    return pl.pallas_call(
        functools.partial(_moba_prompt_kernel, blk=blk, span_blocks=span_blocks, scale=dh ** -0.5,
                          n_sel=min(MOBA_TOPK, nb - 1)),
        out_shape=jax.ShapeDtypeStruct((s, n_heads * dh), BF16),
        grid=(n_heads, s // tq),
        in_specs=[pl.BlockSpec((tq, dh), lambda h, i: (i, cq + h)),
                  pl.BlockSpec((s, dh), lambda h, i: (0, ck + h)),
                  pl.BlockSpec((s, dh), lambda h, i: (0, cv + h))],
        out_specs=pl.BlockSpec((tq, dh), lambda h, i: (i, h)),
        scratch_shapes=[pltpu.VMEM((LANES, dh), F32), pltpu.VMEM((s, 2 * LANES), BF16),
                        pltpu.VMEM((s, 2 * LANES), BF16), pltpu.VMEM((tq, LANES), F32),
                        pltpu.VMEM((tq, 2 * LANES), F32)],
        compiler_params=_params("arbitrary", "arbitrary"),
        name="moba_prompt",
    )(h_all, h_all, h_all)


def _moba_sample_kernel(pt_ref, q_ref, *refs, pps, ppb, n_heads, t_pad, n_new, scale, n_sel):
    k_refs, v_refs = refs[:pps], refs[pps:2 * pps]
    kn_ref, vn_ref, o_ref, m_ref, l_ref, sc_ref, part_ref = refs[2 * pps:]
    step = pl.program_id(1)
    page = kn_ref.shape[1] // n_heads
    lane = lax.broadcasted_iota(jnp.int32, (t_pad, LANES), 1)

    @pl.when(step == 0)
    def _():
        m_ref[...] = jnp.full(m_ref.shape, NEG_INF, F32)
        l_ref[...] = jnp.zeros(l_ref.shape, F32)
        sc_ref[...] = jnp.full(sc_ref.shape, NEG_INF, F32)

    blocks = pps // ppb
    blk = ppb * page
    s_raw = [_dot_nt(q_ref[0, h], jnp.concatenate(
        [_head_rows(r, (0, 0), h, n_heads, page) for r in k_refs], 0).astype(BF16)) for h in range(n_heads)]
    m_new, l_new, sc_new = ([m_ref[h] for h in range(n_heads)], [l_ref[h] for h in range(n_heads)],
                            [sc_ref[h] for h in range(n_heads)])
    probs = []
    for h in range(n_heads):
        for j in range(blocks):
            s_blk = s_raw[h][:, j * blk:(j + 1) * blk]
            m_n = jnp.max(s_blk, axis=-1, keepdims=True) * scale
            probs.append(jnp.exp(s_blk * scale - m_n).astype(BF16))
            here = lane == step * blocks + j
            m_new[h] = jnp.where(here, m_n, m_new[h])
            sc_new[h] = jnp.where(here, jnp.mean(s_blk, axis=-1, keepdims=True), sc_new[h])
    parts = []
    for h in range(n_heads):
        for j in range(blocks):
            vb = jnp.concatenate([_head_rows(v_refs[g], (0, 0), h, n_heads, page)
                                  for g in range(j * ppb, (j + 1) * ppb)], 0).astype(BF16)
            part = _dot(probs[h * blocks + j], _with_ones(vb))
            parts.append(part[:, :LANES])
            l_new[h] = jnp.where(lane == step * blocks + j, part[:, LANES:], l_new[h])
    part_ref[step] = jnp.concatenate(parts, axis=0)
    for h in range(n_heads):
        m_ref[h], l_ref[h], sc_ref[h] = m_new[h], l_new[h], sc_new[h]

    @pl.when(step == pl.num_programs(1) - 1)
    def _():
        for h in range(n_heads):
            qb = q_ref[0, h]
            sel = _topk_mask(sc_ref[h], lane, n_sel) > 0.5
            s_own = _dot_nt(qb, _head_rows(kn_ref, (0,), h, n_heads, page).astype(BF16)) * scale
            s_own = jnp.where(_new_token_mask(t_pad, page, t_pad, n_new), s_own, NEG_INF)
            m_blocks = m_ref[h]
            m_all = jnp.maximum(jnp.max(jnp.where(sel, m_blocks, NEG_INF), axis=-1, keepdims=True),
                                jnp.max(s_own, axis=-1, keepdims=True))
            p_own = jnp.exp(s_own - m_all)
            w = jnp.where(sel, jnp.exp(m_blocks - m_all), 0.0)
            l_all = jnp.sum(w * l_ref[h], axis=-1, keepdims=True) + jnp.sum(p_own, axis=-1, keepdims=True)
            acc = _dot(p_own.astype(BF16), _head_rows(vn_ref, (0,), h, n_heads, page).astype(BF16))
            for b in range(part_ref.shape[0] * blocks):
                row = (h * blocks + b % blocks) * t_pad
                acc = acc + w[:, b:b + 1] * part_ref[b // blocks, row:row + t_pad, :]
            o_ref[0, :, h * LANES:(h + 1) * LANES] = (acc / l_all).astype(BF16)


def _moba_sample(layer, q, k_new, v_new, cache_k, cache_v, page_table, n_heads, n_new, pps):
    b, _, t_pad, dh = q.shape
    page = cache_k.shape[2] // n_heads
    n_pages = page_table.shape[1]
    ppb = MOBA_BLOCK // page
    nb = n_pages // ppb
    assert ppb * page == MOBA_BLOCK and n_pages % pps == 0 and pps % ppb == 0 and nb <= LANES and dh == LANES

    def page_spec(g):
        return pl.BlockSpec((1, 1, page * n_heads, dh), lambda i, n, pt: (layer, pt[i, n * pps + g], 0, 0))

    new_spec = pl.BlockSpec((1, page * n_heads, dh), lambda i, n, pt: (i, 0, 0))
    stat = pltpu.VMEM((n_heads, t_pad, LANES), F32)
    grid_spec = pltpu.PrefetchScalarGridSpec(
        num_scalar_prefetch=1,
        grid=(b, n_pages // pps),
        in_specs=[pl.BlockSpec((1, n_heads, t_pad, dh), lambda i, n, pt: (i, 0, 0, 0))]
                 + [page_spec(g) for g in range(pps)] * 2 + [new_spec, new_spec],
        out_specs=pl.BlockSpec((1, t_pad, n_heads * dh), lambda i, n, pt: (i, 0, 0)),
        scratch_shapes=[stat, stat, stat,
                        pltpu.VMEM((n_pages // pps, (pps // ppb) * n_heads * t_pad, dh), F32)],
    )
    return pl.pallas_call(
        functools.partial(_moba_sample_kernel, pps=pps, ppb=ppb, n_heads=n_heads, t_pad=t_pad, n_new=n_new,
                          scale=dh ** -0.5, n_sel=min(MOBA_TOPK, nb)),
        out_shape=jax.ShapeDtypeStruct((b, t_pad, n_heads * dh), BF16),
        grid_spec=grid_spec,
        compiler_params=_params("parallel", "arbitrary"),
        name="moba_sample",
    )(page_table, q, *([cache_k] * pps), *([cache_v] * pps), k_new, v_new)


def _diff_lambda(lqk, lam_init):
    s1 = jnp.sum(lqk[0:1] * lqk[1:2], axis=-1, keepdims=True)
    s2 = jnp.sum(lqk[2:3] * lqk[3:4], axis=-1, keepdims=True)
    return jnp.exp(s1) - jnp.exp(s2) + lam_init


def _diff_finish(o1, o2, lqk, g, lam_init):
    od = o1 - _diff_lambda(lqk, lam_init) * o2
    od = od * lax.rsqrt(jnp.mean(od * od, axis=-1, keepdims=True) + NORM_EPS) * g
    return od * (1.0 - lam_init)


def _split_sub_heads(q):
    first = lax.broadcasted_iota(jnp.int32, q.shape, q.ndim - 1) < LANES // 2
    return jnp.concatenate([jnp.where(first, q, 0.0), jnp.where(first, 0.0, q)], axis=q.ndim - 2)


def _diff_prompt_kernel(q_ref, k_ref, v_ref, g_ref, lqk_ref, o_ref, kb_ref, vext_ref, m_ref, acc_ref,
                        *, tile, scale, lam_init):
    i = pl.program_id(1)
    s_len = k_ref.shape[0]

    @pl.when(i == 0)
    def _():
        for c in range(s_len // tile):
            rows = slice(c * tile, (c + 1) * tile)
            kb_ref[rows, :] = k_ref[rows, :].astype(BF16)
            vext_ref[rows, :] = _with_ones(v_ref[rows, :].astype(BF16))

    fold = _is_pow2(scale)
    qs = _split_sub_heads(q_ref[...])
    qs = (qs * scale if fold else qs).astype(BF16)

    def scores(rows):
        s = _dot_nt(qs, kb_ref[rows, :])
        return s if fold else s * scale

    _flash_reset(m_ref, acc_ref)
    diag = pl.ds(pl.multiple_of(i * tile, tile), tile)
    s = scores(diag)
    row = lax.broadcasted_iota(jnp.int32, s.shape, 0) % tile
    col = lax.broadcasted_iota(jnp.int32, s.shape, 1)
    _flash_update(jnp.where(col <= row, s, NEG_INF), vext_ref[diag, :], m_ref, acc_ref)

    _flash_over_tiles(i, lambda n: pl.ds(pl.multiple_of(n * tile, tile), tile), scores,
                      lambda rows: vext_ref[rows, :], m_ref, acc_ref)
    o = _flash_result(acc_ref)
    o_ref[...] = _diff_finish(o[:tile], o[tile:], lqk_ref[...], g_ref[...], lam_init).astype(BF16)


def _diff_prompt(h_all, cols, lqk, subln_g, lam_init, n_heads, dh, tile):
    s = h_all.shape[0]
    w = 2 * dh
    assert w == LANES and s % tile == 0
    cq, ck, cv = cols["dq"] // w, cols["dk"] // w, cols["dv"] // w
    return pl.pallas_call(
        functools.partial(_diff_prompt_kernel, tile=tile, scale=dh ** -0.5, lam_init=lam_init),
        out_shape=jax.ShapeDtypeStruct((s, n_heads * w), BF16),
        grid=(n_heads, s // tile),
        in_specs=[pl.BlockSpec((tile, w), lambda h, i: (i, cq + h)),
                  pl.BlockSpec((s, w), lambda h, i: (0, ck + h)),
                  pl.BlockSpec((s, w), lambda h, i: (0, cv + h)),
                  pl.BlockSpec((1, w), lambda h, i: (0, 0)),
                  pl.BlockSpec(lqk.shape, lambda h, i: (0, 0))],
        out_specs=pl.BlockSpec((tile, w), lambda h, i: (i, h)),
        scratch_shapes=[pltpu.VMEM((s, w), BF16), pltpu.VMEM((s, 2 * LANES), BF16),
                        pltpu.VMEM((2 * tile, LANES), F32), pltpu.VMEM((2 * tile, 2 * LANES), F32)],
        compiler_params=_params("arbitrary", "arbitrary"),
        name="diff_prompt",
    )(h_all, h_all, h_all, subln_g.reshape(1, w), lqk)


def _diff_sample_kernel(pt_ref, q_ref, *refs, pps, n_heads, t_pad, n_new, lam_init):
    k_refs, v_refs = refs[:pps], refs[pps:2 * pps]
    kn_ref, vn_ref, g_ref, lqk_ref, o_ref, m_ref, acc_ref = refs[2 * pps:]
    step = pl.program_id(1)
    page = kn_ref.shape[2]

    def head_dims(h):
        return slice(h * LANES, (h + 1) * LANES)

    @pl.when(step == 0)
    def _():
        _flash_reset(m_ref, acc_ref)
        for h in range(n_heads):
            s = _dot(q_ref[0, h], kn_ref[0, head_dims(h), :].astype(BF16))
            s = jnp.where(_new_token_mask(2 * t_pad, page, t_pad, n_new), s, NEG_INF)
            _flash_update(s, _with_ones(_head_rows(vn_ref, (0,), h, n_heads, page).astype(BF16)),
                          m_ref.at[h], acc_ref.at[h])

    scores = [_dot(q_ref[0, h], jnp.concatenate([r[0, 0, head_dims(h), :] for r in k_refs], 1).astype(BF16))
              for h in range(n_heads)]
    weights = [_flash_weights(scores[h], m_ref.at[h]) for h in range(n_heads)]
    for h in range(n_heads):
        vb = jnp.concatenate([_head_rows(r, (0, 0), h, n_heads, page) for r in v_refs], 0).astype(BF16)
        _flash_accumulate(*weights[h], _with_ones(vb), acc_ref.at[h])

    @pl.when(step == pl.num_programs(1) - 1)
    def _():
        for h in range(n_heads):
            o = _flash_result(acc_ref.at[h])
            o_ref[0, :, h * LANES:(h + 1) * LANES] = _diff_finish(
                o[:t_pad], o[t_pad:], lqk_ref[...], g_ref[...], lam_init).astype(BF16)


def _diff_sample(layer, q, k_new, v_new, cache_k, cache_v, page_table, lqk, subln_g, lam_init,
                 n_heads, n_new, pps):
    b, _, rows, w = q.shape
    t_pad = rows // 2
    page = cache_k.shape[3]
    n_pages = page_table.shape[1]
    assert n_pages % pps == 0 and w == LANES and page == LANES

    def page_spec(g, shape):
        return pl.BlockSpec((1, 1) + shape, lambda i, n, pt: (layer, pt[i, n * pps + g], 0, 0))

    def new_spec(shape):
        return pl.BlockSpec((1,) + shape, lambda i, n, pt: (i, 0, 0))

    k_shape, v_shape = (n_heads * w, page), (page * n_heads, w)
    grid_spec = pltpu.PrefetchScalarGridSpec(
        num_scalar_prefetch=1,
        grid=(b, n_pages // pps),
        in_specs=[pl.BlockSpec((1, n_heads, rows, w), lambda i, n, pt: (i, 0, 0, 0))]
                 + [page_spec(g, k_shape) for g in range(pps)] + [page_spec(g, v_shape) for g in range(pps)]
                 + [new_spec(k_shape), new_spec(v_shape),
                    pl.BlockSpec((1, LANES), lambda i, n, pt: (0, 0)),
                    pl.BlockSpec(lqk.shape, lambda i, n, pt: (0, 0))],
        out_specs=pl.BlockSpec((1, t_pad, n_heads * w), lambda i, n, pt: (i, 0, 0)),
        scratch_shapes=[pltpu.VMEM((n_heads, rows, LANES), F32), pltpu.VMEM((n_heads, rows, 2 * LANES), F32)],
    )
    return pl.pallas_call(
        functools.partial(_diff_sample_kernel, pps=pps, n_heads=n_heads, t_pad=t_pad, n_new=n_new,
                          lam_init=lam_init),
        out_shape=jax.ShapeDtypeStruct((b, t_pad, n_heads * w), BF16),
        grid_spec=grid_spec,
        compiler_params=_params("parallel", "arbitrary"),
        name="diff_sample",
    )(page_table, q, *([cache_k] * pps), *([cache_v] * pps), k_new, v_new, subln_g.reshape(1, LANES), lqk)


def _merge_kernel(x_ref, yr_ref, om_ref, od_ref, gates_ref, wr_ref, wm_ref, wd_ref, wo_ref, o_ref):
    d = x_ref.shape[1]
    mix = (jax.nn.sigmoid(gates_ref[:, 0:d]) * _dot(yr_ref[...], wr_ref[...])
           + jax.nn.sigmoid(gates_ref[:, d:2 * d]) * _dot(om_ref[...], wm_ref[...])
           + jax.nn.sigmoid(gates_ref[:, 2 * d:3 * d]) * _dot(od_ref[...], wd_ref[...]))
    o_ref[...] = x_ref[...] + _dot(mix.astype(BF16), wo_ref[...])


def _merge(x, y_ret, o_moba, o_diff, h_all, gate_col, w_ret, w_moba, w_diff, w_out, layer, tm):
    m, d = x.shape
    assert gate_col % (3 * d) == 0

    def rows(w):
        return pl.BlockSpec((tm, w), lambda i: (i, 0))

    def whole(arr):
        return pl.BlockSpec((None,) + arr.shape[1:], lambda i: (layer, 0, 0), pipeline_mode=pl.Buffered(1))

    return pl.pallas_call(
        _merge_kernel,
        out_shape=jax.ShapeDtypeStruct((m, d), F32),
        grid=(m // tm,),
        in_specs=[rows(d), rows(y_ret.shape[1]), rows(o_moba.shape[1]), rows(o_diff.shape[1]),
                  pl.BlockSpec((tm, 3 * d), lambda i: (i, gate_col // (3 * d))),
                  whole(w_ret), whole(w_moba), whole(w_diff), whole(w_out)],
        out_specs=rows(d),
        compiler_params=_params("parallel"),
        name="merge_out",
    )(x, y_ret, o_moba, o_diff, h_all, w_ret, w_moba, w_diff, w_out)


def _ffn_kernel(x_ref, g_ref, w1_ref, w2_ref, fg_ref, o_ref, xn_ref, *, final_norm):
    t = pl.program_id(1)

    @pl.when(t == 0)
    def _():
        x = x_ref[...]
        xn_ref[...] = _rms(x, g_ref[...]).astype(BF16)
        o_ref[...] = x

    hid = jnp.square(jnp.maximum(_dot(xn_ref[...], w1_ref[...]), 0.0))
    o_ref[...] += _dot(hid.astype(BF16), w2_ref[...])

    if final_norm:
        @pl.when(t == pl.num_programs(1) - 1)
        def _():
            o_ref[...] = _rms(o_ref[...], fg_ref[...])


def _ffn(x, g, w1, w2, layer, final_g, final_norm, tm, tf):
    m, d = x.shape
    f = w1.shape[2]
    return pl.pallas_call(
        functools.partial(_ffn_kernel, final_norm=final_norm),
        out_shape=jax.ShapeDtypeStruct((m, d), F32),
        grid=(m // tm, f // tf),
        in_specs=[pl.BlockSpec((tm, d), lambda i, t: (i, 0)),
                  pl.BlockSpec((1, d), lambda i, t: (0, 0)),
                  pl.BlockSpec((None, d, tf), lambda i, t: (layer, 0, t)),
                  pl.BlockSpec((None, tf, d), lambda i, t: (layer, t, 0)),
                  pl.BlockSpec((1, d), lambda i, t: (0, 0))],
        out_specs=pl.BlockSpec((tm, d), lambda i, t: (i, 0)),
        scratch_shapes=[pltpu.VMEM((tm, d), BF16)],
        compiler_params=_params("parallel", "arbitrary"),
        name="ffn",
    )(x, g.reshape(1, d), w1, w2, final_g.reshape(1, d))


def _pad_rows(x, rows):
    return jnp.pad(x, ((0, 0), (0, rows - x.shape[1]), (0, 0)))


def _per_head(x, n_heads, t_pad):
    b = x.shape[0]
    return _pad_rows(x, t_pad).reshape(b, t_pad, n_heads, LANES).transpose(0, 2, 1, 3)


def _as_page(x, n_heads, page):
    return _pad_rows(x, page).reshape(x.shape[0], page * n_heads, LANES)


def kernel(x_prompt, x_sample, cache_moba_k, cache_moba_v, cache_diff_k, cache_diff_v, state_ret, page_table, norm1_g, w_in, diff_lq1, diff_lk1, diff_lq2, diff_lk2, diff_subln_g, w_up_ret, w_up_moba, w_up_diff, w_out, norm2_g, w_ff1, w_ff2, final_g):
    bp, s_len, d = x_prompt.shape
    bs, t_new, _ = x_sample.shape
    depth, n_pool, page, moba_h, moba_dh = cache_moba_k.shape
    diff_h, diff_dh = cache_diff_k.shape[3], cache_diff_k.shape[5]
    ret_h, ret_dk, ret_dv = state_ret.shape[2:]
    n_pages = page_table.shape[1]
    past_len = n_pages * page
    assert bp == 1 and ret_dk == LANES and moba_dh == LANES and 2 * diff_dh == LANES
    assert t_new <= SUBLANES and past_len % MOBA_BLOCK == 0

    ret_qk_w, ret_v_w = ret_h * ret_dk, ret_h * ret_dv
    moba_w, diff_w = moba_h * moba_dh, diff_h * 2 * diff_dh
    names = ["rq", "rk", "rv", "rg", "mq", "mk", "mv", "dq", "dk", "dv", "gr", "gm", "gd"]
    widths = [ret_qk_w, ret_qk_w, ret_v_w, ret_v_w, moba_w, moba_w, moba_w, diff_w, diff_w, diff_w, d, d, d]
    cols, off = {}, 0
    for nm, wd in zip(names, widths):
        cols[nm] = off
        off += wd
    in_w = off
    assert w_in.shape == (depth, d, in_w)

    tn = 512
    kinds = {"rq": EPI_ROPE128, "rk": EPI_ROPE128_SCALED, "mq": EPI_ROPE128, "mk": EPI_ROPE128,
             "dq": EPI_ROPE64, "dk": EPI_ROPE64}
    tile_kinds = []
    for nm, wd in zip(names, widths):
        assert wd % tn == 0
        tile_kinds += [kinds.get(nm, EPI_NONE)] * (wd // tn)
    tile_kinds = tuple(tile_kinds)
    assert moba_w == tn and diff_w == tn and moba_h == diff_h
    copy_tiles = tuple(cols[nm] // tn for nm in ("mk", "mv", "dv"))

    m_s = bs * t_new
    xp = x_prompt.reshape(s_len, d)
    xs = x_sample.reshape(m_s, d)
    tables_p = _rope_tables(np.arange(s_len))
    tables_s = _rope_tables(np.tile(past_len + np.arange(t_new), bs))
    tm_p = min(1024, s_len)
    tm_ffn = min(512, s_len)
    tm_merge = min(256, s_len)
    tf = min(512, w_ff1.shape[-1])
    diff_tile = min(512, s_len)
    ret_chunk = min(256, s_len)
    t_pad = SUBLANES
    pps = min(8, n_pages)

    ck_moba = cache_moba_k.reshape(depth, n_pool, page * moba_h, moba_dh)
    cv_moba = cache_moba_v.reshape(depth, n_pool, page * moba_h, moba_dh)
    ck_diff = cache_diff_k.transpose(0, 1, 3, 4, 5, 2).reshape(depth, n_pool, diff_w, page)
    cv_diff = cache_diff_v.reshape(depth, n_pool, page * diff_h, 2 * diff_dh)

    w_in_b, w_ret_b, w_moba_b, w_diff_b, w_out_b, w_ff1_b, w_ff2_b = (
        w.astype(BF16) for w in (w_in, w_up_ret, w_up_moba, w_up_diff, w_out, w_ff1, w_ff2))

    outs = {k: [] for k in ["pmk", "pmv", "pdk", "pdv", "pst", "smk", "smv", "sdk", "sdv", "sst"]}
    for l in range(depth):
        lam_init = 0.8 - 0.6 * math.exp(-0.3 * l)
        lqk = jnp.stack([diff_lq1[l], diff_lk1[l], diff_lq2[l], diff_lk2[l]])
        last = l == depth - 1
        k_scale = ret_dk ** -0.5

        def sec(h_all, nm, wd):
            return h_all[:, cols[nm]:cols[nm] + wd]

        hp, pmk, pmv, pdv = _in_proj(xp, norm1_g[l], w_in_b, l, tables_p, tile_kinds, copy_tiles, k_scale,
                                     tm_p, tn)
        y_ret, s_fin = _ret_prompt(hp, cols, ret_h, ret_dk, ret_dv, ret_chunk)
        o_moba = _moba_prompt(hp, cols, moba_h, moba_dh)
        o_diff = _diff_prompt(hp, cols, lqk, diff_subln_g[l], lam_init, diff_h, diff_dh, diff_tile)
        x1 = _merge(xp, y_ret, o_moba, o_diff, hp, cols["gr"], w_ret_b, w_moba_b, w_diff_b, w_out_b, l, tm_merge)
        xp = _ffn(x1, norm2_g[l], w_ff1_b, w_ff2_b, l, final_g, last, tm_ffn, tf)
        outs["pmk"].append(pmk.reshape(bp, s_len, moba_h, moba_dh))
        outs["pmv"].append(pmv.reshape(bp, s_len, moba_h, moba_dh))
        outs["pdk"].append(sec(hp, "dk", diff_w).reshape(bp, s_len, diff_h, 2, diff_dh))
        outs["pdv"].append(pdv.reshape(bp, s_len, diff_h, 2 * diff_dh))
        outs["pst"].append(s_fin.reshape(bp, ret_h, ret_dk, ret_dv))

        hs = _in_proj(xs, norm1_g[l], w_in_b, l, tables_s, tile_kinds, (), k_scale, m_s, tn)[0]

        def sec3(nm, wd):
            return sec(hs, nm, wd).reshape(bs, t_new, wd)

        y_ret_s, s_new = _ret_sample(
            _pad_rows(sec3("rq", ret_qk_w), t_pad), _pad_rows(sec3("rk", ret_qk_w), t_pad),
            _pad_rows(sec3("rv", ret_v_w), t_pad), _pad_rows(sec3("rg", ret_v_w), t_pad),
            state_ret[l], ret_h, ret_dk, ret_dv, t_new)
        mk_s, mv_s = sec3("mk", moba_w), sec3("mv", moba_w)
        o_moba_s = _moba_sample(l, _per_head(sec3("mq", moba_w), moba_h, t_pad).astype(BF16),
                                _as_page(mk_s, moba_h, page), _as_page(mv_s, moba_h, page),
                                ck_moba, cv_moba, page_table, moba_h, t_new, pps)
        dk_s, dv_s = sec3("dk", diff_w), sec3("dv", diff_w)
        diff_scale = diff_dh ** -0.5
        assert _is_pow2(diff_scale)
        dq_s = _split_sub_heads(_per_head(sec3("dq", diff_w) * diff_scale, diff_h, t_pad)).astype(BF16)
        o_diff_s = _diff_sample(l, dq_s, _pad_rows(dk_s, page).transpose(0, 2, 1), _as_page(dv_s, diff_h, page),
                                ck_diff, cv_diff, page_table, lqk, diff_subln_g[l], lam_init,
                                diff_h, t_new, pps)

        def unpad(o):
            return o[:, :t_new].reshape(m_s, o.shape[-1])

        x1s = _merge(xs, unpad(y_ret_s), unpad(o_moba_s), unpad(o_diff_s), hs, cols["gr"],
                     w_ret_b, w_moba_b, w_diff_b, w_out_b, l, m_s)
        xs = _ffn(x1s, norm2_g[l], w_ff1_b, w_ff2_b, l, final_g, last, m_s, tf)
        outs["smk"].append(mk_s.reshape(bs, t_new, moba_h, moba_dh))
        outs["smv"].append(mv_s.reshape(bs, t_new, moba_h, moba_dh))
        outs["sdk"].append(dk_s.reshape(bs, t_new, diff_h, 2, diff_dh))
        outs["sdv"].append(dv_s.reshape(bs, t_new, diff_h, 2 * diff_dh))
        outs["sst"].append(s_new)

    st = {k: jnp.stack(v) for k, v in outs.items()}
    return (xp.reshape(bp, s_len, d), xs.reshape(bs, t_new, d),
            st["pmk"], st["pmv"], st["pdk"], st["pdv"], st["pst"],
            st["smk"], st["smv"], st["sdk"], st["sdv"], st["sst"])
```

```python
import functools
import math

import numpy as np
import jax
import jax.numpy as jnp
from jax import lax
from jax.experimental import pallas as pl
from jax.experimental.pallas import tpu as pltpu

NORM_EPS = 1e-6
NEG_INF = -1e30
BELOW_NEG_INF = -3e38
ROPE_THETA = 10000.0
MOBA_BLOCK = 256
MOBA_TOPK = 3
LANES = 128
SUBLANES = 8
VMEM_LIMIT_BYTES = 48 * 1024 * 1024
LOG2_E = 1.4426950408889634

BF16 = jnp.bfloat16
F32 = jnp.float32


def _dot(a, b):
    return jnp.dot(a, b, preferred_element_type=F32)


def _dot_nt(a, b, precision=None):
    return lax.dot_general(a, b, (((1,), (1,)), ((), ())), precision=precision,
                           preferred_element_type=F32)


def _dot_tn(a, b):
    return lax.dot_general(a, b, (((0,), (0,)), ((), ())), preferred_element_type=F32)


def _params(*sem):
    return pltpu.CompilerParams(dimension_semantics=sem, vmem_limit_bytes=VMEM_LIMIT_BYTES)


def _rms(x, g):
    return x * lax.rsqrt(jnp.mean(x * x, axis=-1, keepdims=True) + NORM_EPS) * g


def _is_pow2(v):
    m, _ = math.frexp(v)
    return m == 0.5


EPI_NONE, EPI_ROPE128, EPI_ROPE128_SCALED, EPI_ROPE64 = 0, 1, 2, 3


def _in_proj_kernel(x_ref, g_ref, w_ref, ca_ref, sa_ref, cb_ref, sb1_ref, sb2_ref, o_ref, *rest,
                    tile_kinds, copy_tiles, k_scale):
    copy_refs, xn_ref = rest[:-1], rest[-1]
    j = pl.program_id(1)
    tm = o_ref.shape[0]

    @pl.when(j == 0)
    def _():
        xn_ref[...] = _rms(x_ref[...], g_ref[...]).astype(BF16)

    for rows in ([slice(0, tm // 2), slice(tm // 2, tm)] if tm % (2 * SUBLANES) == 0 else [slice(0, tm)]):
        o_ref[rows, :] = _dot(xn_ref[rows, :], w_ref[...])
    n_groups = o_ref.shape[1] // LANES

    def among(kind):
        js = [t for t, kd in enumerate(tile_kinds) if kd == kind]
        cond = j == js[0]
        for t in js[1:]:
            cond = cond | (j == t)
        return cond

    def rope128(scale):
        ca, sa = ca_ref[...], sa_ref[...]
        for c in range(n_groups):
            seg = o_ref[:, c * LANES:(c + 1) * LANES]
            out = seg * ca + pltpu.roll(seg, LANES // 2, axis=1) * sa
            o_ref[:, c * LANES:(c + 1) * LANES] = out if scale is None else out * scale

    @pl.when(among(EPI_ROPE128))
    def _():
        rope128(None)

    @pl.when(among(EPI_ROPE128_SCALED))
    def _():
        rope128(k_scale)

    @pl.when(among(EPI_ROPE64))
    def _():
        cb, sb1, sb2 = cb_ref[...], sb1_ref[...], sb2_ref[...]
        for c in range(n_groups):
            seg = o_ref[:, c * LANES:(c + 1) * LANES]
            up = pltpu.roll(seg, LANES - LANES // 4, axis=1)
            down = pltpu.roll(seg, LANES // 4, axis=1)
            o_ref[:, c * LANES:(c + 1) * LANES] = seg * cb + up * sb1 + down * sb2

    for copy_ref, tile in zip(copy_refs, copy_tiles):
        @pl.when(j == tile)
        def _(copy_ref=copy_ref):
            for c in range(n_groups):
                copy_ref[pl.ds(c, tm, stride=n_groups), :] = o_ref[:, c * LANES:(c + 1) * LANES]


def _rope_tables(pos):
    f32 = np.float32
    pos = np.asarray(pos, np.float64)[:, None]
    half_a, half_b = LANES // 2, LANES // 4
    inv_a = ROPE_THETA ** (-np.arange(half_a, dtype=np.float64) / half_a)
    inv_b = ROPE_THETA ** (-np.arange(half_b, dtype=np.float64) / half_b)
    ang_a = pos * inv_a[None, :]
    ang_b = pos * inv_b[None, :]
    cos_a, sin_a = np.cos(ang_a), np.sin(ang_a)
    cos_b, sin_b = np.cos(ang_b), np.sin(ang_b)
    zero_b = np.zeros_like(sin_b)
    ca = np.concatenate([cos_a, cos_a], -1)
    sa = np.concatenate([-sin_a, sin_a], -1)
    cb = np.concatenate([cos_b] * 4, -1)
    sb1 = np.concatenate([-sin_b, zero_b] * 2, -1)
    sb2 = np.concatenate([zero_b, sin_b] * 2, -1)
    return tuple(jnp.asarray(t.astype(f32)) for t in (ca, sa, cb, sb1, sb2))


def _in_proj(x, g, w_all, layer, tables, tile_kinds, copy_tiles, k_scale, tm, tn):
    m, d = x.shape
    n = w_all.shape[2]
    groups = tn // LANES
    tab_spec = pl.BlockSpec((tm, LANES), lambda i, j: (i, 0))
    copy_shape = jax.ShapeDtypeStruct((m * groups, LANES), F32)
    copy_spec = pl.BlockSpec((tm * groups, LANES), lambda i, j: (i, 0))
    return pl.pallas_call(
        functools.partial(_in_proj_kernel, tile_kinds=tile_kinds, copy_tiles=copy_tiles, k_scale=k_scale),
        out_shape=(jax.ShapeDtypeStruct((m, n), F32),) + (copy_shape,) * len(copy_tiles),
        grid=(m // tm, n // tn),
        in_specs=[pl.BlockSpec((tm, d), lambda i, j: (i, 0)),
                  pl.BlockSpec((1, d), lambda i, j: (0, 0)),
                  pl.BlockSpec((None, d, tn), lambda i, j: (layer, 0, j)),
                  tab_spec, tab_spec, tab_spec, tab_spec, tab_spec],
        out_specs=(pl.BlockSpec((tm, tn), lambda i, j: (i, j)),) + (copy_spec,) * len(copy_tiles),
        scratch_shapes=[pltpu.VMEM((tm, d), BF16)],
        compiler_params=_params("parallel", "arbitrary"),
        name="in_proj",
    )(x, g.reshape(1, d), w_all, *tables)


def _ret_log_g(h):
    return math.log(1.0 - 2.0 ** (-5.0 - h))


def _ret_heads(q_all, k_all, v_all, rg_all, get_state, set_state, write_y, *, n_heads, dk, dv, chunk):
    rows = q_all.shape[0]
    ri = lax.broadcasted_iota(jnp.int32, (rows, rows), 0)
    ci = lax.broadcasted_iota(jnp.int32, (rows, rows), 1)
    diff = (ri - ci).astype(F32)
    pos = lax.broadcasted_iota(jnp.int32, (rows, 1), 0).astype(F32)
    for h in range(n_heads):
        lg = _ret_log_g(h)
        q = q_all[:, h * dk:(h + 1) * dk]
        k = k_all[:, h * dk:(h + 1) * dk]
        vb = v_all[:, h * dv:(h + 1) * dv].astype(BF16)
        qb = q.astype(BF16)
        dmask = jnp.where(diff >= 0, jnp.exp(jnp.maximum(diff, 0.0) * lg), 0.0)
        a = _dot_nt(qb, k.astype(BF16)) * dmask
        state = get_state(h)
        o = _dot(a.astype(BF16), vb) + _dot(qb, state.astype(BF16)) * jnp.exp((pos + 1.0) * lg)
        kd = (k * jnp.exp((chunk - 1.0 - pos) * lg)).astype(BF16)
        set_state(h, state * math.exp(chunk * lg) + _dot_tn(kd, vb))
        yn = o * lax.rsqrt(jnp.mean(o * o, axis=-1, keepdims=True) + NORM_EPS)
        rg = rg_all[:, h * dv:(h + 1) * dv]
        write_y(h, (yn * (rg * jax.nn.sigmoid(rg))).astype(BF16))


def _ret_prompt_kernel(q_ref, k_ref, v_ref, rg_ref, y_ref, s_ref, *, n_heads, dk, dv, chunk):
    @pl.when(pl.program_id(0) == 0)
    def _():
        s_ref[...] = jnp.zeros_like(s_ref)

    def set_state(h, val):
        s_ref[h] = val

    def write_y(h, val):
        y_ref[:, h * dv:(h + 1) * dv] = val

    _ret_heads(q_ref[...], k_ref[...], v_ref[...], rg_ref[...], lambda h: s_ref[h], set_state, write_y,
               n_heads=n_heads, dk=dk, dv=dv, chunk=chunk)


def _ret_prompt(h_all, cols, n_heads, dk, dv, chunk):
    s = h_all.shape[0]
    qk_w, v_w = n_heads * dk, n_heads * dv
    return pl.pallas_call(
        functools.partial(_ret_prompt_kernel, n_heads=n_heads, dk=dk, dv=dv, chunk=chunk),
        out_shape=(jax.ShapeDtypeStruct((s, v_w), BF16),
                   jax.ShapeDtypeStruct((n_heads, dk, dv), F32)),
        grid=(s // chunk,),
        in_specs=[pl.BlockSpec((chunk, qk_w), lambda c: (c, cols["rq"] // qk_w)),
                  pl.BlockSpec((chunk, qk_w), lambda c: (c, cols["rk"] // qk_w)),
                  pl.BlockSpec((chunk, v_w), lambda c: (c, cols["rv"] // v_w)),
                  pl.BlockSpec((chunk, v_w), lambda c: (c, cols["rg"] // v_w))],
        out_specs=(pl.BlockSpec((chunk, v_w), lambda c: (c, 0)),
                   pl.BlockSpec((n_heads, dk, dv), lambda c: (0, 0, 0))),
        compiler_params=_params("arbitrary"),
        name="ret_prompt",
    )(h_all, h_all, h_all, h_all)


def _ret_sample_kernel(q_ref, k_ref, v_ref, rg_ref, s0_ref, y_ref, s_ref, *, n_heads, dk, dv, chunk):
    def set_state(h, val):
        s_ref[0, h] = val

    def write_y(h, val):
        y_ref[0, :, h * dv:(h + 1) * dv] = val

    _ret_heads(q_ref[0], k_ref[0], v_ref[0], rg_ref[0], lambda h: s0_ref[0, h], set_state, write_y,
               n_heads=n_heads, dk=dk, dv=dv, chunk=chunk)


def _ret_sample(q, k, v, rg, state, n_heads, dk, dv, chunk):
    b, rows, _ = q.shape
    qk_w, v_w = n_heads * dk, n_heads * dv

    def row_spec(w):
        return pl.BlockSpec((1, rows, w), lambda i: (i, 0, 0))

    st_spec = pl.BlockSpec((1, n_heads, dk, dv), lambda i: (i, 0, 0, 0))
    return pl.pallas_call(
        functools.partial(_ret_sample_kernel, n_heads=n_heads, dk=dk, dv=dv, chunk=chunk),
        out_shape=(jax.ShapeDtypeStruct((b, rows, v_w), BF16),
                   jax.ShapeDtypeStruct(state.shape, F32)),
        grid=(b,),
        in_specs=[row_spec(qk_w), row_spec(qk_w), row_spec(v_w), row_spec(v_w), st_spec],
        out_specs=(row_spec(v_w), st_spec),
        compiler_params=_params("parallel"),
        name="ret_sample",
    )(q, k, v, rg, state)


def _lane_tile(x, n):
    return x if n == 1 else jnp.concatenate([x] * n, axis=1)


def _flash_update(s, v_ext, m_ref, acc_ref, scale=None):
    p, alpha = _flash_weights(s, m_ref, scale)
    _flash_accumulate(p, alpha, v_ext, acc_ref)


def _flash_weights(s, m_ref, scale=None):
    m_old = m_ref[...]
    m_new = jnp.maximum(m_old, jnp.max(s, axis=-1, keepdims=True))
    m_wide = _lane_tile(m_new, s.shape[1] // LANES)
    if scale is None:
        alpha = jnp.exp(m_old - m_new)
        p = jnp.exp(s - m_wide)
    else:
        alpha = jnp.exp2((m_old - m_new) * (scale * LOG2_E))
        p = jnp.exp2((s - m_wide) * (scale * LOG2_E))
    m_ref[...] = m_new
    return p.astype(BF16), alpha


def _flash_accumulate(p, alpha, v_ext, acc_ref):
    acc_ref[...] = _lane_tile(alpha, 2) * acc_ref[...] + _dot(p, v_ext)


def _flash_over_tiles(lead_rows, lead_mask, n_tiles, tile_rows, score_fn, value_fn, m_ref, acc_ref, scale=None):
    def two(rows_a, rows_b, mask_a=None):
        s_a, s_b = score_fn(rows_a), score_fn(rows_b)
        if mask_a is not None:
            s_a = jnp.where(mask_a, s_a, NEG_INF)
        _flash_update(s_a, value_fn(rows_a), m_ref, acc_ref, scale)
        _flash_update(s_b, value_fn(rows_b), m_ref, acc_ref, scale)

    @pl.when(n_tiles % 2 == 1)
    def _():
        two(lead_rows, tile_rows(n_tiles - 1), lead_mask)

    @pl.when(n_tiles % 2 == 0)
    def _():
        _flash_update(jnp.where(lead_mask, score_fn(lead_rows), NEG_INF), value_fn(lead_rows), m_ref, acc_ref,
                      scale)

    def pair(c, carry):
        two(tile_rows(2 * c), tile_rows(2 * c + 1))
        return carry

    lax.fori_loop(0, n_tiles // 2, pair, 0)


def _flash_reset(m_ref, acc_ref):
    m_ref[...] = jnp.full(m_ref.shape, NEG_INF, F32)
    acc_ref[...] = jnp.zeros(acc_ref.shape, F32)


def _flash_result(acc_ref):
    acc = acc_ref[...]
    return acc[:, :LANES] / acc[:, LANES:]


def _with_ones(v_bf16):
    return jnp.concatenate([v_bf16, jnp.ones(v_bf16.shape, BF16)], axis=1)


def _new_token_mask(rows, keys, t_pad, n_new):
    r = lax.broadcasted_iota(jnp.int32, (rows, keys), 0) % t_pad
    c = lax.broadcasted_iota(jnp.int32, (rows, keys), 1)
    return (c <= r) & (c < n_new)


def _head_rows(ref, idx, h, n_heads, page):
    return ref[idx + (pl.ds(h, page, stride=n_heads), slice(None))]


def _topk_mask(sc, lane, n_sel):
    sel = jnp.zeros(sc.shape, F32)
    for _ in range(n_sel):
        mx = jnp.max(sc, axis=-1, keepdims=True)
        idx = jnp.min(jnp.where(sc == mx, lane, LANES), axis=-1, keepdims=True)
        pick = lane == idx
        sel = jnp.where(pick & (mx > 0.5 * NEG_INF), 1.0, sel)
        sc = jnp.where(pick, BELOW_NEG_INF, sc)
    return sel


def _moba_prompt_kernel(q_ref, k_ref, v_ref, o_ref, kmean_ref, kext_ref, vext_ref, m_ref, acc_ref,
                        *, blk, span_blocks, scale, n_sel):
    i = pl.program_id(1)
    tq = q_ref.shape[0]
    s_len, dh = k_ref.shape
    nb = s_len // blk
    span = span_blocks * blk

    @pl.when(i == 0)
    def _():
        kmean_ref[...] = jnp.zeros_like(kmean_ref)
        kmean_ref[0:nb, :] = jnp.mean(k_ref[...].reshape(nb, blk, dh), axis=1)
        blk_lane = lax.broadcasted_iota(jnp.int32, (blk, LANES), 1)
        for n in range(nb):
            rows = slice(n * blk, (n + 1) * blk)
            kext_ref[rows, 0:LANES] = k_ref[rows, :].astype(BF16)
            kext_ref[rows, LANES:] = jnp.where(blk_lane == n, 1.0, 0.0).astype(BF16)
            vext_ref[rows, :] = _with_ones(v_ref[rows, :].astype(BF16))

    q = q_ref[...]
    lane = lax.broadcasted_iota(jnp.int32, (tq, LANES), 1)
    own = (i * tq + lax.broadcasted_iota(jnp.int32, (tq, LANES), 0)) // blk
    sc = _dot_nt(q, kmean_ref[...], precision=lax.Precision.HIGHEST)
    sel = _topk_mask(jnp.where(lane < own, sc, NEG_INF), lane, n_sel)
    sel = jnp.where(lane == own, 1.0, sel)
    q_ext = jnp.concatenate([q.astype(BF16), ((1.0 - sel) * NEG_INF).astype(BF16)], axis=1)

    _flash_reset(m_ref, acc_ref)
    last = (i * tq) // span
    q_pos = i * tq + lax.broadcasted_iota(jnp.int32, (tq, span), 0)
    k_pos = last * span + lax.broadcasted_iota(jnp.int32, (tq, span), 1)

    def span_rows(c):
        return pl.ds(pl.multiple_of(c * span, span), span)

    _flash_over_tiles(span_rows(last), k_pos <= q_pos, last, span_rows,
                      lambda rows: _dot_nt(q_ext, kext_ref[rows, :]), lambda rows: vext_ref[rows, :],
                      m_ref, acc_ref, scale)
    o_ref[...] = _flash_result(acc_ref).astype(BF16)


def _moba_prompt(h_all, cols, n_heads, dh):
    s = h_all.shape[0]
    blk = MOBA_BLOCK
    nb = s // blk
    span_blocks = math.gcd(nb, 4)
    tq = math.gcd(span_blocks, 2) * blk
    assert s % blk == 0 and nb <= LANES and dh == LANES
    cq, ck, cv = cols["mq"] // dh, cols["mk"] // dh, cols["mv"] // dh
    return pl.pallas_call(
        functools.partial(_moba_prompt_kernel, blk=blk, span_blocks=span_blocks, scale=dh ** -0.5,
                          n_sel=min(MOBA_TOPK, nb - 1)),
        out_shape=jax.ShapeDtypeStruct((s, n_heads * dh), BF16),
        grid=(n_heads, s // tq),
        in_specs=[pl.BlockSpec((tq, dh), lambda h, i: (i, cq + h)),
                  pl.BlockSpec((s, dh), lambda h, i: (0, ck + h)),
                  pl.BlockSpec((s, dh), lambda h, i: (0, cv + h))],
        out_specs=pl.BlockSpec((tq, dh), lambda h, i: (i, h)),
        scratch_shapes=[pltpu.VMEM((LANES, dh), F32), pltpu.VMEM((s, 2 * LANES), BF16),
                        pltpu.VMEM((s, 2 * LANES), BF16), pltpu.VMEM((tq, LANES), F32),
                        pltpu.VMEM((tq, 2 * LANES), F32)],
        compiler_params=_params("arbitrary", "arbitrary"),
        name="moba_prompt",
    )(h_all, h_all, h_all)


def _moba_sample_kernel(pt_ref, q_ref, *refs, pps, ppb, n_heads, t_pad, n_new, scale, n_sel):
    k_refs, v_refs = refs[:pps], refs[pps:2 * pps]
    kn_ref, vn_ref, o_ref, m_ref, l_ref, sc_ref, part_ref = refs[2 * pps:]
    step = pl.program_id(1)
    page = kn_ref.shape[1] // n_heads
    lane = lax.broadcasted_iota(jnp.int32, (t_pad, LANES), 1)

    @pl.when(step == 0)
    def _():
        m_ref[...] = jnp.full(m_ref.shape, NEG_INF, F32)
        l_ref[...] = jnp.zeros(l_ref.shape, F32)
        sc_ref[...] = jnp.full(sc_ref.shape, NEG_INF, F32)

    blocks = pps // ppb
    blk = ppb * page
    s_raw = [_dot_nt(q_ref[0, h], jnp.concatenate(
        [_head_rows(r, (0, 0), h, n_heads, page) for r in k_refs], 0).astype(BF16)) for h in range(n_heads)]
    m_new, l_new, sc_new = ([m_ref[h] for h in range(n_heads)], [l_ref[h] for h in range(n_heads)],
                            [sc_ref[h] for h in range(n_heads)])
    probs = []
    for h in range(n_heads):
        for j in range(blocks):
            s_blk = s_raw[h][:, j * blk:(j + 1) * blk]
            m_n = jnp.max(s_blk, axis=-1, keepdims=True) * scale
            probs.append(jnp.exp(s_blk * scale - m_n).astype(BF16))
            here = lane == step * blocks + j
            m_new[h] = jnp.where(here, m_n, m_new[h])
            sc_new[h] = jnp.where(here, jnp.mean(s_blk, axis=-1, keepdims=True), sc_new[h])
    parts = []
    for h in range(n_heads):
        for j in range(blocks):
            vb = jnp.concatenate([_head_rows(v_refs[g], (0, 0), h, n_heads, page)
                                  for g in range(j * ppb, (j + 1) * ppb)], 0).astype(BF16)
            part = _dot(probs[h * blocks + j], _with_ones(vb))
            parts.append(part[:, :LANES])
            l_new[h] = jnp.where(lane == step * blocks + j, part[:, LANES:], l_new[h])
    part_ref[step] = jnp.concatenate(parts, axis=0)
    for h in range(n_heads):
        m_ref[h], l_ref[h], sc_ref[h] = m_new[h], l_new[h], sc_new[h]

    @pl.when(step == pl.num_programs(1) - 1)
    def _():
        own_mask = _new_token_mask(t_pad, page, t_pad, n_new)
        s_own = [_dot_nt(q_ref[0, h], _head_rows(kn_ref, (0,), h, n_heads, page).astype(BF16))
                 for h in range(n_heads)]
        p_own, w_blocks, l_all = [], [], []
        for h in range(n_heads):
            sel = _topk_mask(sc_ref[h], lane, n_sel) > 0.5
            s = jnp.where(own_mask, s_own[h] * scale, NEG_INF)
            m_blocks = m_ref[h]
            m_all = jnp.maximum(jnp.max(jnp.where(sel, m_blocks, NEG_INF), axis=-1, keepdims=True),
                                jnp.max(s, axis=-1, keepdims=True))
            p = jnp.exp(s - m_all)
            w = jnp.where(sel, jnp.exp(m_blocks - m_all), 0.0)
            l_all.append(jnp.sum(w * l_ref[h], axis=-1, keepdims=True) + jnp.sum(p, axis=-1, keepdims=True))
            p_own.append(p.astype(BF16))
            w_blocks.append(w)
        for h in range(n_heads):
            acc = _dot(p_own[h], _head_rows(vn_ref, (0,), h, n_heads, page).astype(BF16))
            for b in range(part_ref.shape[0] * blocks):
                row = (h * blocks + b % blocks) * t_pad
                acc = acc + w_blocks[h][:, b:b + 1] * part_ref[b // blocks, row:row + t_pad, :]
            o_ref[0, :, h * LANES:(h + 1) * LANES] = (acc / l_all[h]).astype(BF16)


def _moba_sample(layer, q, k_new, v_new, cache_k, cache_v, page_table, n_heads, n_new, pps):
    b, _, t_pad, dh = q.shape
    page = cache_k.shape[2] // n_heads
    n_pages = page_table.shape[1]
    ppb = MOBA_BLOCK // page
    nb = n_pages // ppb
    assert ppb * page == MOBA_BLOCK and n_pages % pps == 0 and pps % ppb == 0 and nb <= LANES and dh == LANES

    def page_spec(g):
        return pl.BlockSpec((1, 1, page * n_heads, dh), lambda i, n, pt: (layer, pt[i, n * pps + g], 0, 0))

    new_spec = pl.BlockSpec((1, page * n_heads, dh), lambda i, n, pt: (i, 0, 0))
    stat = pltpu.VMEM((n_heads, t_pad, LANES), F32)
    grid_spec = pltpu.PrefetchScalarGridSpec(
        num_scalar_prefetch=1,
        grid=(b, n_pages // pps),
        in_specs=[pl.BlockSpec((1, n_heads, t_pad, dh), lambda i, n, pt: (i, 0, 0, 0))]
                 + [page_spec(g) for g in range(pps)] * 2 + [new_spec, new_spec],
        out_specs=pl.BlockSpec((1, t_pad, n_heads * dh), lambda i, n, pt: (i, 0, 0)),
        scratch_shapes=[stat, stat, stat,
                        pltpu.VMEM((n_pages // pps, (pps // ppb) * n_heads * t_pad, dh), F32)],
    )
    return pl.pallas_call(
        functools.partial(_moba_sample_kernel, pps=pps, ppb=ppb, n_heads=n_heads, t_pad=t_pad, n_new=n_new,
                          scale=dh ** -0.5, n_sel=min(MOBA_TOPK, nb)),
        out_shape=jax.ShapeDtypeStruct((b, t_pad, n_heads * dh), BF16),
        grid_spec=grid_spec,
        compiler_params=_params("parallel", "arbitrary"),
        name="moba_sample",
    )(page_table, q, *([cache_k] * pps), *([cache_v] * pps), k_new, v_new)


def _diff_lambda(lqk, lam_init):
    s1 = jnp.sum(lqk[0:1] * lqk[1:2], axis=-1, keepdims=True)
    s2 = jnp.sum(lqk[2:3] * lqk[3:4], axis=-1, keepdims=True)
    return jnp.exp(s1) - jnp.exp(s2) + lam_init


def _diff_finish(o1, o2, lqk, g, lam_init):
    od = o1 - _diff_lambda(lqk, lam_init) * o2
    od = od * lax.rsqrt(jnp.mean(od * od, axis=-1, keepdims=True) + NORM_EPS) * g
    return od * (1.0 - lam_init)


def _split_sub_heads(q):
    first = lax.broadcasted_iota(jnp.int32, q.shape, q.ndim - 1) < LANES // 2
    return jnp.concatenate([jnp.where(first, q, 0.0), jnp.where(first, 0.0, q)], axis=q.ndim - 2)


def _diff_prompt_kernel(q_ref, k_ref, v_ref, g_ref, lqk_ref, o_ref, kb_ref, vext_ref, m_ref, acc_ref,
                        *, tile, scale, lam_init):
    i = pl.program_id(1)
    s_len = k_ref.shape[0]

    @pl.when(i == 0)
    def _():
        for c in range(s_len // tile):
            rows = slice(c * tile, (c + 1) * tile)
            kb_ref[rows, :] = k_ref[rows, :].astype(BF16)
            vext_ref[rows, :] = _with_ones(v_ref[rows, :].astype(BF16))

    fold = _is_pow2(scale)
    qs = _split_sub_heads(q_ref[...])
    qs = (qs * scale if fold else qs).astype(BF16)

    def scores(rows):
        s = _dot_nt(qs, kb_ref[rows, :])
        return s if fold else s * scale

    _flash_reset(m_ref, acc_ref)
    row = lax.broadcasted_iota(jnp.int32, (2 * tile, tile), 0) % tile
    col = lax.broadcasted_iota(jnp.int32, (2 * tile, tile), 1)

    def tile_rows(n):
        return pl.ds(pl.multiple_of(n * tile, tile), tile)

    _flash_over_tiles(tile_rows(i), col <= row, i, tile_rows, scores, lambda rows: vext_ref[rows, :],
                      m_ref, acc_ref)
    o = _flash_result(acc_ref)
    o_ref[...] = _diff_finish(o[:tile], o[tile:], lqk_ref[...], g_ref[...], lam_init).astype(BF16)


def _diff_prompt(h_all, cols, lqk, subln_g, lam_init, n_heads, dh, tile):
    s = h_all.shape[0]
    w = 2 * dh
    assert w == LANES and s % tile == 0
    cq, ck, cv = cols["dq"] // w, cols["dk"] // w, cols["dv"] // w
    return pl.pallas_call(
        functools.partial(_diff_prompt_kernel, tile=tile, scale=dh ** -0.5, lam_init=lam_init),
        out_shape=jax.ShapeDtypeStruct((s, n_heads * w), BF16),
        grid=(n_heads, s // tile),
        in_specs=[pl.BlockSpec((tile, w), lambda h, i: (i, cq + h)),
                  pl.BlockSpec((s, w), lambda h, i: (0, ck + h)),
                  pl.BlockSpec((s, w), lambda h, i: (0, cv + h)),
                  pl.BlockSpec((1, w), lambda h, i: (0, 0)),
                  pl.BlockSpec(lqk.shape, lambda h, i: (0, 0))],
        out_specs=pl.BlockSpec((tile, w), lambda h, i: (i, h)),
        scratch_shapes=[pltpu.VMEM((s, w), BF16), pltpu.VMEM((s, 2 * LANES), BF16),
                        pltpu.VMEM((2 * tile, LANES), F32), pltpu.VMEM((2 * tile, 2 * LANES), F32)],
        compiler_params=_params("arbitrary", "arbitrary"),
        name="diff_prompt",
    )(h_all, h_all, h_all, subln_g.reshape(1, w), lqk)


def _diff_sample_kernel(pt_ref, q_ref, *refs, pps, n_heads, t_pad, n_new, lam_init):
    k_refs, v_refs = refs[:pps], refs[pps:2 * pps]
    kn_ref, vn_ref, g_ref, lqk_ref, o_ref, m_ref, acc_ref = refs[2 * pps:]
    step = pl.program_id(1)
    page = kn_ref.shape[2]

    def head_dims(h):
        return slice(h * LANES, (h + 1) * LANES)

    @pl.when(step == 0)
    def _():
        _flash_reset(m_ref, acc_ref)
        new_mask = _new_token_mask(2 * t_pad, page, t_pad, n_new)
        new_s = [_dot(q_ref[0, h], kn_ref[0, head_dims(h), :].astype(BF16)) for h in range(n_heads)]
        new_w = [_flash_weights(jnp.where(new_mask, new_s[h], NEG_INF), m_ref.at[h]) for h in range(n_heads)]
        for h in range(n_heads):
            _flash_accumulate(*new_w[h], _with_ones(_head_rows(vn_ref, (0,), h, n_heads, page).astype(BF16)),
                              acc_ref.at[h])

    scores = [_dot(q_ref[0, h], jnp.concatenate([r[0, 0, head_dims(h), :] for r in k_refs], 1).astype(BF16))
              for h in range(n_heads)]
    weights = [_flash_weights(scores[h], m_ref.at[h]) for h in range(n_heads)]
    for h in range(n_heads):
        vb = jnp.concatenate([_head_rows(r, (0, 0), h, n_heads, page) for r in v_refs], 0).astype(BF16)
        _flash_accumulate(*weights[h], _with_ones(vb), acc_ref.at[h])

    @pl.when(step == pl.num_programs(1) - 1)
    def _():
        for h in range(n_heads):
            o = _flash_result(acc_ref.at[h])
            o_ref[0, :, h * LANES:(h + 1) * LANES] = _diff_finish(
                o[:t_pad], o[t_pad:], lqk_ref[...], g_ref[...], lam_init).astype(BF16)


def _diff_sample(layer, q, k_new, v_new, cache_k, cache_v, page_table, lqk, subln_g, lam_init,
                 n_heads, n_new, pps):
    b, _, rows, w = q.shape
    t_pad = rows // 2
    page = cache_k.shape[3]
    n_pages = page_table.shape[1]
    assert n_pages % pps == 0 and w == LANES and page == LANES

    def page_spec(g, shape):
        return pl.BlockSpec((1, 1) + shape, lambda i, n, pt: (layer, pt[i, n * pps + g], 0, 0))

    def new_spec(shape):
        return pl.BlockSpec((1,) + shape, lambda i, n, pt: (i, 0, 0))

    k_shape, v_shape = (n_heads * w, page), (page * n_heads, w)
    grid_spec = pltpu.PrefetchScalarGridSpec(
        num_scalar_prefetch=1,
        grid=(b, n_pages // pps),
        in_specs=[pl.BlockSpec((1, n_heads, rows, w), lambda i, n, pt: (i, 0, 0, 0))]
                 + [page_spec(g, k_shape) for g in range(pps)] + [page_spec(g, v_shape) for g in range(pps)]
                 + [new_spec(k_shape), new_spec(v_shape),
                    pl.BlockSpec((1, LANES), lambda i, n, pt: (0, 0)),
                    pl.BlockSpec(lqk.shape, lambda i, n, pt: (0, 0))],
        out_specs=pl.BlockSpec((1, t_pad, n_heads * w), lambda i, n, pt: (i, 0, 0)),
        scratch_shapes=[pltpu.VMEM((n_heads, rows, LANES), F32), pltpu.VMEM((n_heads, rows, 2 * LANES), F32)],
    )
    return pl.pallas_call(
        functools.partial(_diff_sample_kernel, pps=pps, n_heads=n_heads, t_pad=t_pad, n_new=n_new,
                          lam_init=lam_init),
        out_shape=jax.ShapeDtypeStruct((b, t_pad, n_heads * w), BF16),
        grid_spec=grid_spec,
        compiler_params=_params("parallel", "arbitrary"),
        name="diff_sample",
    )(page_table, q, *([cache_k] * pps), *([cache_v] * pps), k_new, v_new, subln_g.reshape(1, LANES), lqk)


def _merge_kernel(x_ref, yr_ref, om_ref, od_ref, gates_ref, wr_ref, wm_ref, wd_ref, wo_ref, o_ref):
    d = x_ref.shape[1]
    mix = (jax.nn.sigmoid(gates_ref[:, 0:d]) * _dot(yr_ref[...], wr_ref[...])
           + jax.nn.sigmoid(gates_ref[:, d:2 * d]) * _dot(om_ref[...], wm_ref[...])
           + jax.nn.sigmoid(gates_ref[:, 2 * d:3 * d]) * _dot(od_ref[...], wd_ref[...]))
    o_ref[...] = x_ref[...] + _dot(mix.astype(BF16), wo_ref[...])


def _merge(x, y_ret, o_moba, o_diff, h_all, gate_col, w_ret, w_moba, w_diff, w_out, layer, tm):
    m, d = x.shape
    assert gate_col % (3 * d) == 0

    def rows(w):
        return pl.BlockSpec((tm, w), lambda i: (i, 0))

    def whole(arr):
        return pl.BlockSpec((None,) + arr.shape[1:], lambda i: (layer, 0, 0), pipeline_mode=pl.Buffered(1))

    return pl.pallas_call(
        _merge_kernel,
        out_shape=jax.ShapeDtypeStruct((m, d), F32),
        grid=(m // tm,),
        in_specs=[rows(d), rows(y_ret.shape[1]), rows(o_moba.shape[1]), rows(o_diff.shape[1]),
                  pl.BlockSpec((tm, 3 * d), lambda i: (i, gate_col // (3 * d))),
                  whole(w_ret), whole(w_moba), whole(w_diff), whole(w_out)],
        out_specs=rows(d),
        compiler_params=_params("parallel"),
        name="merge_out",
    )(x, y_ret, o_moba, o_diff, h_all, w_ret, w_moba, w_diff, w_out)


def _ffn_kernel(x_ref, g_ref, w1_ref, w2_ref, fg_ref, o_ref, xn_ref, *, final_norm):
    t = pl.program_id(1)

    @pl.when(t == 0)
    def _():
        x = x_ref[...]
        xn_ref[...] = _rms(x, g_ref[...]).astype(BF16)
        o_ref[...] = x

    hid = jnp.square(jnp.maximum(_dot(xn_ref[...], w1_ref[...]), 0.0))
    o_ref[...] += _dot(hid.astype(BF16), w2_ref[...])

    if final_norm:
        @pl.when(t == pl.num_programs(1) - 1)
        def _():
            o_ref[...] = _rms(o_ref[...], fg_ref[...])


def _ffn(x, g, w1, w2, layer, final_g, final_norm, tm, tf):
    m, d = x.shape
    f = w1.shape[2]
    return pl.pallas_call(
        functools.partial(_ffn_kernel, final_norm=final_norm),
        out_shape=jax.ShapeDtypeStruct((m, d), F32),
        grid=(m // tm, f // tf),
        in_specs=[pl.BlockSpec((tm, d), lambda i, t: (i, 0)),
                  pl.BlockSpec((1, d), lambda i, t: (0, 0)),
                  pl.BlockSpec((None, d, tf), lambda i, t: (layer, 0, t)),
                  pl.BlockSpec((None, tf, d), lambda i, t: (layer, t, 0)),
                  pl.BlockSpec((1, d), lambda i, t: (0, 0))],
        out_specs=pl.BlockSpec((tm, d), lambda i, t: (i, 0)),
        scratch_shapes=[pltpu.VMEM((tm, d), BF16)],
        compiler_params=_params("parallel", "arbitrary"),
        name="ffn",
    )(x, g.reshape(1, d), w1, w2, final_g.reshape(1, d))


def _pad_rows(x, rows):
    return jnp.pad(x, ((0, 0), (0, rows - x.shape[1]), (0, 0)))


def _per_head(x, n_heads, t_pad):
    b = x.shape[0]
    return _pad_rows(x, t_pad).reshape(b, t_pad, n_heads, LANES).transpose(0, 2, 1, 3)


def _as_page(x, n_heads, page):
    return _pad_rows(x, page).reshape(x.shape[0], page * n_heads, LANES)


def kernel(x_prompt, x_sample, cache_moba_k, cache_moba_v, cache_diff_k, cache_diff_v, state_ret, page_table, norm1_g, w_in, diff_lq1, diff_lk1, diff_lq2, diff_lk2, diff_subln_g, w_up_ret, w_up_moba, w_up_diff, w_out, norm2_g, w_ff1, w_ff2, final_g):
    bp, s_len, d = x_prompt.shape
    bs, t_new, _ = x_sample.shape
    depth, n_pool, page, moba_h, moba_dh = cache_moba_k.shape
    diff_h, diff_dh = cache_diff_k.shape[3], cache_diff_k.shape[5]
    ret_h, ret_dk, ret_dv = state_ret.shape[2:]
    n_pages = page_table.shape[1]
    past_len = n_pages * page
    assert bp == 1 and ret_dk == LANES and moba_dh == LANES and 2 * diff_dh == LANES
    assert t_new <= SUBLANES and past_len % MOBA_BLOCK == 0

    ret_qk_w, ret_v_w = ret_h * ret_dk, ret_h * ret_dv
    moba_w, diff_w = moba_h * moba_dh, diff_h * 2 * diff_dh
    names = ["rq", "rk", "rv", "rg", "mq", "mk", "mv", "dq", "dk", "dv", "gr", "gm", "gd"]
    widths = [ret_qk_w, ret_qk_w, ret_v_w, ret_v_w, moba_w, moba_w, moba_w, diff_w, diff_w, diff_w, d, d, d]
    cols, off = {}, 0
    for nm, wd in zip(names, widths):
        cols[nm] = off
        off += wd
    in_w = off
    assert w_in.shape == (depth, d, in_w)

    tn = 512
    kinds = {"rq": EPI_ROPE128, "rk": EPI_ROPE128_SCALED, "mq": EPI_ROPE128, "mk": EPI_ROPE128,
             "dq": EPI_ROPE64, "dk": EPI_ROPE64}
    tile_kinds = []
    for nm, wd in zip(names, widths):
        assert wd % tn == 0
        tile_kinds += [kinds.get(nm, EPI_NONE)] * (wd // tn)
    tile_kinds = tuple(tile_kinds)
    assert moba_w == tn and diff_w == tn and moba_h == diff_h
    copy_tiles = tuple(cols[nm] // tn for nm in ("mk", "mv", "dv"))

    m_s = bs * t_new
    xp = x_prompt.reshape(s_len, d)
    xs = x_sample.reshape(m_s, d)
    tables_p = _rope_tables(np.arange(s_len))
    tables_s = _rope_tables(np.tile(past_len + np.arange(t_new), bs))
    tm_p = min(1024, s_len)
    tm_ffn = min(512, s_len)
    tm_merge = min(256, s_len)
    tf = min(512, w_ff1.shape[-1])
    diff_tile = min(512, s_len)
    ret_chunk = min(256, s_len)
    t_pad = SUBLANES
    pps = min(8, n_pages)

    ck_moba = cache_moba_k.reshape(depth, n_pool, page * moba_h, moba_dh)
    cv_moba = cache_moba_v.reshape(depth, n_pool, page * moba_h, moba_dh)
    ck_diff = cache_diff_k.transpose(0, 1, 3, 4, 5, 2).reshape(depth, n_pool, diff_w, page)
    cv_diff = cache_diff_v.reshape(depth, n_pool, page * diff_h, 2 * diff_dh)

    w_in_b, w_ret_b, w_moba_b, w_diff_b, w_out_b, w_ff1_b, w_ff2_b = (
        w.astype(BF16) for w in (w_in, w_up_ret, w_up_moba, w_up_diff, w_out, w_ff1, w_ff2))

    outs = {k: [] for k in ["pmk", "pmv", "pdk", "pdv", "pst", "smk", "smv", "sdk", "sdv", "sst"]}
    for l in range(depth):
        lam_init = 0.8 - 0.6 * math.exp(-0.3 * l)
        lqk = jnp.stack([diff_lq1[l], diff_lk1[l], diff_lq2[l], diff_lk2[l]])
        last = l == depth - 1
        k_scale = ret_dk ** -0.5

        def sec(h_all, nm, wd):
            return h_all[:, cols[nm]:cols[nm] + wd]

        hp, pmk, pmv, pdv = _in_proj(xp, norm1_g[l], w_in_b, l, tables_p, tile_kinds, copy_tiles, k_scale,
                                     tm_p, tn)
        y_ret, s_fin = _ret_prompt(hp, cols, ret_h, ret_dk, ret_dv, ret_chunk)
        o_moba = _moba_prompt(hp, cols, moba_h, moba_dh)
        o_diff = _diff_prompt(hp, cols, lqk, diff_subln_g[l], lam_init, diff_h, diff_dh, diff_tile)
        x1 = _merge(xp, y_ret, o_moba, o_diff, hp, cols["gr"], w_ret_b, w_moba_b, w_diff_b, w_out_b, l, tm_merge)
        xp = _ffn(x1, norm2_g[l], w_ff1_b, w_ff2_b, l, final_g, last, tm_ffn, tf)
        outs["pmk"].append(pmk.reshape(bp, s_len, moba_h, moba_dh))
        outs["pmv"].append(pmv.reshape(bp, s_len, moba_h, moba_dh))
        outs["pdk"].append(sec(hp, "dk", diff_w).reshape(bp, s_len, diff_h, 2, diff_dh))
        outs["pdv"].append(pdv.reshape(bp, s_len, diff_h, 2 * diff_dh))
        outs["pst"].append(s_fin.reshape(bp, ret_h, ret_dk, ret_dv))

        hs = _in_proj(xs, norm1_g[l], w_in_b, l, tables_s, tile_kinds, (), k_scale, m_s, tn)[0]

        def sec3(nm, wd):
            return sec(hs, nm, wd).reshape(bs, t_new, wd)

        y_ret_s, s_new = _ret_sample(
            _pad_rows(sec3("rq", ret_qk_w), t_pad), _pad_rows(sec3("rk", ret_qk_w), t_pad),
            _pad_rows(sec3("rv", ret_v_w), t_pad), _pad_rows(sec3("rg", ret_v_w), t_pad),
            state_ret[l], ret_h, ret_dk, ret_dv, t_new)
        mk_s, mv_s = sec3("mk", moba_w), sec3("mv", moba_w)
        o_moba_s = _moba_sample(l, _per_head(sec3("mq", moba_w), moba_h, t_pad).astype(BF16),
                                _as_page(mk_s, moba_h, page), _as_page(mv_s, moba_h, page),
                                ck_moba, cv_moba, page_table, moba_h, t_new, pps)
        dk_s, dv_s = sec3("dk", diff_w), sec3("dv", diff_w)
        diff_scale = diff_dh ** -0.5
        assert _is_pow2(diff_scale)
        dq_s = _split_sub_heads(_per_head(sec3("dq", diff_w) * diff_scale, diff_h, t_pad)).astype(BF16)
        o_diff_s = _diff_sample(l, dq_s, _pad_rows(dk_s, page).transpose(0, 2, 1), _as_page(dv_s, diff_h, page),
                                ck_diff, cv_diff, page_table, lqk, diff_subln_g[l], lam_init,
                                diff_h, t_new, pps)

        def unpad(o):
            return o[:, :t_new].reshape(m_s, o.shape[-1])

        x1s = _merge(xs, unpad(y_ret_s), unpad(o_moba_s), unpad(o_diff_s), hs, cols["gr"],
                     w_ret_b, w_moba_b, w_diff_b, w_out_b, l, m_s)
        xs = _ffn(x1s, norm2_g[l], w_ff1_b, w_ff2_b, l, final_g, last, m_s, tf)
        outs["smk"].append(mk_s.reshape(bs, t_new, moba_h, moba_dh))
        outs["smv"].append(mv_s.reshape(bs, t_new, moba_h, moba_dh))
        outs["sdk"].append(dk_s.reshape(bs, t_new, diff_h, 2, diff_dh))
        outs["sdv"].append(dv_s.reshape(bs, t_new, diff_h, 2 * diff_dh))
        outs["sst"].append(s_new)

    st = {k: jnp.stack(v) for k, v in outs.items()}
    return (xp.reshape(bp, s_len, d), xs.reshape(bs, t_new, d),
            st["pmk"], st["pmv"], st["pdk"], st["pdv"], st["pst"],
            st["smk"], st["smv"], st["sdk"], st["sdv"], st["sst"])
```

```python
import functools
import math

import numpy as np
import jax
import jax.numpy as jnp
from jax import lax
from jax.experimental import pallas as pl
from jax.experimental.pallas import tpu as pltpu

NORM_EPS = 1e-6
NEG_INF = -1e30
BELOW_NEG_INF = -3e38
ROPE_THETA = 10000.0
MOBA_BLOCK = 256
MOBA_TOPK = 3
LANES = 128
SUBLANES = 8
VMEM_LIMIT_BYTES = 48 * 1024 * 1024
LOG2_E = 1.4426950408889634

BF16 = jnp.bfloat16
F32 = jnp.float32


def _dot(a, b):
    return jnp.dot(a, b, preferred_element_type=F32)


def _dot_nt(a, b, precision=None):
    return lax.dot_general(a, b, (((1,), (1,)), ((), ())), precision=precision,
                           preferred_element_type=F32)


def _dot_tn(a, b):
    return lax.dot_general(a, b, (((0,), (0,)), ((), ())), preferred_element_type=F32)


def _params(*sem):
    return pltpu.CompilerParams(dimension_semantics=sem, vmem_limit_bytes=VMEM_LIMIT_BYTES)


def _rms(x, g):
    return x * lax.rsqrt(jnp.mean(x * x, axis=-1, keepdims=True) + NORM_EPS) * g


def _is_pow2(v):
    m, _ = math.frexp(v)
    return m == 0.5


EPI_NONE, EPI_ROPE128, EPI_ROPE128_SCALED, EPI_ROPE64 = 0, 1, 2, 3


def _in_proj_kernel(x_ref, g_ref, w_ref, ca_ref, sa_ref, cb_ref, sb1_ref, sb2_ref, o_ref, *rest,
                    tile_kinds, copy_tiles, k_scale):
    copy_refs, xn_ref = rest[:-1], rest[-1]
    j = pl.program_id(1)
    tm = o_ref.shape[0]

    @pl.when(j == 0)
    def _():
        xn_ref[...] = _rms(x_ref[...], g_ref[...]).astype(BF16)

    for rows in ([slice(0, tm // 2), slice(tm // 2, tm)] if tm % (2 * SUBLANES) == 0 else [slice(0, tm)]):
        o_ref[rows, :] = _dot(xn_ref[rows, :], w_ref[...])
    n_groups = o_ref.shape[1] // LANES

    def among(kind):
        js = [t for t, kd in enumerate(tile_kinds) if kd == kind]
        cond = j == js[0]
        for t in js[1:]:
            cond = cond | (j == t)
        return cond

    def rope128(scale):
        ca, sa = ca_ref[...], sa_ref[...]
        for c in range(n_groups):
            seg = o_ref[:, c * LANES:(c + 1) * LANES]
            out = seg * ca + pltpu.roll(seg, LANES // 2, axis=1) * sa
            o_ref[:, c * LANES:(c + 1) * LANES] = out if scale is None else out * scale

    @pl.when(among(EPI_ROPE128))
    def _():
        rope128(None)

    @pl.when(among(EPI_ROPE128_SCALED))
    def _():
        rope128(k_scale)

    @pl.when(among(EPI_ROPE64))
    def _():
        cb, sb1, sb2 = cb_ref[...], sb1_ref[...], sb2_ref[...]
        for c in range(n_groups):
            seg = o_ref[:, c * LANES:(c + 1) * LANES]
            up = pltpu.roll(seg, LANES - LANES // 4, axis=1)
            down = pltpu.roll(seg, LANES // 4, axis=1)
            o_ref[:, c * LANES:(c + 1) * LANES] = seg * cb + up * sb1 + down * sb2

    for copy_ref, tile in zip(copy_refs, copy_tiles):
        @pl.when(j == tile)
        def _(copy_ref=copy_ref):
            for c in range(n_groups):
                copy_ref[pl.ds(c, tm, stride=n_groups), :] = o_ref[:, c * LANES:(c + 1) * LANES]


def _rope_tables(pos):
    f32 = np.float32
    pos = np.asarray(pos, np.float64)[:, None]
    half_a, half_b = LANES // 2, LANES // 4
    inv_a = ROPE_THETA ** (-np.arange(half_a, dtype=np.float64) / half_a)
    inv_b = ROPE_THETA ** (-np.arange(half_b, dtype=np.float64) / half_b)
    ang_a = pos * inv_a[None, :]
    ang_b = pos * inv_b[None, :]
    cos_a, sin_a = np.cos(ang_a), np.sin(ang_a)
    cos_b, sin_b = np.cos(ang_b), np.sin(ang_b)
    zero_b = np.zeros_like(sin_b)
    ca = np.concatenate([cos_a, cos_a], -1)
    sa = np.concatenate([-sin_a, sin_a], -1)
    cb = np.concatenate([cos_b] * 4, -1)
    sb1 = np.concatenate([-sin_b, zero_b] * 2, -1)
    sb2 = np.concatenate([zero_b, sin_b] * 2, -1)
    return tuple(jnp.asarray(t.astype(f32)) for t in (ca, sa, cb, sb1, sb2))


def _in_proj(x, g, w_all, layer, tables, tile_kinds, copy_tiles, k_scale, tm, tn):
    m, d = x.shape
    n = w_all.shape[2]
    groups = tn // LANES
    tab_spec = pl.BlockSpec((tm, LANES), lambda i, j: (i, 0))
    copy_shape = jax.ShapeDtypeStruct((m * groups, LANES), F32)
    copy_spec = pl.BlockSpec((tm * groups, LANES), lambda i, j: (i, 0))
    return pl.pallas_call(
        functools.partial(_in_proj_kernel, tile_kinds=tile_kinds, copy_tiles=copy_tiles, k_scale=k_scale),
        out_shape=(jax.ShapeDtypeStruct((m, n), F32),) + (copy_shape,) * len(copy_tiles),
        grid=(m // tm, n // tn),
        in_specs=[pl.BlockSpec((tm, d), lambda i, j: (i, 0)),
                  pl.BlockSpec((1, d), lambda i, j: (0, 0)),
                  pl.BlockSpec((None, d, tn), lambda i, j: (layer, 0, j)),
                  tab_spec, tab_spec, tab_spec, tab_spec, tab_spec],
        out_specs=(pl.BlockSpec((tm, tn), lambda i, j: (i, j)),) + (copy_spec,) * len(copy_tiles),
        scratch_shapes=[pltpu.VMEM((tm, d), BF16)],
        compiler_params=_params("parallel", "arbitrary"),
        name="in_proj",
    )(x, g.reshape(1, d), w_all, *tables)


def _ret_log_g(h):
    return math.log(1.0 - 2.0 ** (-5.0 - h))


def _ret_heads(q_all, k_all, v_all, rg_all, get_state, set_state, write_y, *, n_heads, dk, dv, chunk):
    rows = q_all.shape[0]
    ri = lax.broadcasted_iota(jnp.int32, (rows, rows), 0)
    ci = lax.broadcasted_iota(jnp.int32, (rows, rows), 1)
    diff = (ri - ci).astype(F32)
    pos = lax.broadcasted_iota(jnp.int32, (rows, 1), 0).astype(F32)
    for h in range(n_heads):
        lg = _ret_log_g(h)
        q = q_all[:, h * dk:(h + 1) * dk]
        k = k_all[:, h * dk:(h + 1) * dk]
        vb = v_all[:, h * dv:(h + 1) * dv].astype(BF16)
        qb = q.astype(BF16)
        dmask = jnp.where(diff >= 0, jnp.exp(jnp.maximum(diff, 0.0) * lg), 0.0)
        a = _dot_nt(qb, k.astype(BF16)) * dmask
        state = get_state(h)
        o = _dot(a.astype(BF16), vb) + _dot(qb, state.astype(BF16)) * jnp.exp((pos + 1.0) * lg)
        kd = (k * jnp.exp((chunk - 1.0 - pos) * lg)).astype(BF16)
        set_state(h, state * math.exp(chunk * lg) + _dot_tn(kd, vb))
        yn = o * lax.rsqrt(jnp.mean(o * o, axis=-1, keepdims=True) + NORM_EPS)
        rg = rg_all[:, h * dv:(h + 1) * dv]
        write_y(h, (yn * (rg * jax.nn.sigmoid(rg))).astype(BF16))


def _ret_prompt_kernel(q_ref, k_ref, v_ref, rg_ref, y_ref, s_ref, *, n_heads, dk, dv, chunk):
    @pl.when(pl.program_id(0) == 0)
    def _():
        s_ref[...] = jnp.zeros_like(s_ref)

    def set_state(h, val):
        s_ref[h] = val

    def write_y(h, val):
        y_ref[:, h * dv:(h + 1) * dv] = val

    _ret_heads(q_ref[...], k_ref[...], v_ref[...], rg_ref[...], lambda h: s_ref[h], set_state, write_y,
               n_heads=n_heads, dk=dk, dv=dv, chunk=chunk)


def _ret_prompt(h_all, cols, n_heads, dk, dv, chunk):
    s = h_all.shape[0]
    qk_w, v_w = n_heads * dk, n_heads * dv
    return pl.pallas_call(
        functools.partial(_ret_prompt_kernel, n_heads=n_heads, dk=dk, dv=dv, chunk=chunk),
        out_shape=(jax.ShapeDtypeStruct((s, v_w), BF16),
                   jax.ShapeDtypeStruct((n_heads, dk, dv), F32)),
        grid=(s // chunk,),
        in_specs=[pl.BlockSpec((chunk, qk_w), lambda c: (c, cols["rq"] // qk_w)),
                  pl.BlockSpec((chunk, qk_w), lambda c: (c, cols["rk"] // qk_w)),
                  pl.BlockSpec((chunk, v_w), lambda c: (c, cols["rv"] // v_w)),
                  pl.BlockSpec((chunk, v_w), lambda c: (c, cols["rg"] // v_w))],
        out_specs=(pl.BlockSpec((chunk, v_w), lambda c: (c, 0)),
                   pl.BlockSpec((n_heads, dk, dv), lambda c: (0, 0, 0))),
        compiler_params=_params("arbitrary"),
        name="ret_prompt",
    )(h_all, h_all, h_all, h_all)


def _ret_sample_kernel(q_ref, k_ref, v_ref, rg_ref, s0_ref, y_ref, s_ref, *, n_heads, dk, dv, chunk):
    def set_state(h, val):
        s_ref[0, h] = val

    def write_y(h, val):
        y_ref[0, :, h * dv:(h + 1) * dv] = val

    _ret_heads(q_ref[0], k_ref[0], v_ref[0], rg_ref[0], lambda h: s0_ref[0, h], set_state, write_y,
               n_heads=n_heads, dk=dk, dv=dv, chunk=chunk)


def _ret_sample(q, k, v, rg, state_all, layer, n_heads, dk, dv, chunk):
    b, rows, _ = q.shape
    qk_w, v_w = n_heads * dk, n_heads * dv

    def row_spec(w):
        return pl.BlockSpec((1, rows, w), lambda i: (i, 0, 0))

    return pl.pallas_call(
        functools.partial(_ret_sample_kernel, n_heads=n_heads, dk=dk, dv=dv, chunk=chunk),
        out_shape=(jax.ShapeDtypeStruct((b, rows, v_w), BF16),
                   jax.ShapeDtypeStruct(state_all.shape[1:], F32)),
        grid=(b,),
        in_specs=[row_spec(qk_w), row_spec(qk_w), row_spec(v_w), row_spec(v_w),
                  pl.BlockSpec((None, 1, n_heads, dk, dv), lambda i: (layer, i, 0, 0, 0))],
        out_specs=(row_spec(v_w), pl.BlockSpec((1, n_heads, dk, dv), lambda i: (i, 0, 0, 0))),
        compiler_params=_params("parallel"),
        name="ret_sample",
    )(q, k, v, rg, state_all)


def _lane_tile(x, n):
    return x if n == 1 else jnp.concatenate([x] * n, axis=1)


def _flash_update(s, v_ext, m_ref, acc_ref, scale=None):
    p, alpha = _flash_weights(s, m_ref, scale)
    _flash_accumulate(p, alpha, v_ext, acc_ref)


def _flash_weights(s, m_ref, scale=None):
    m_old = m_ref[...]
    m_new = jnp.maximum(m_old, jnp.max(s, axis=-1, keepdims=True))
    m_wide = _lane_tile(m_new, s.shape[1] // LANES)
    if scale is None:
        alpha = jnp.exp(m_old - m_new)
        p = jnp.exp(s - m_wide)
    else:
        alpha = jnp.exp2((m_old - m_new) * (scale * LOG2_E))
        p = jnp.exp2((s - m_wide) * (scale * LOG2_E))
    m_ref[...] = m_new
    return p.astype(BF16), alpha


def _flash_accumulate(p, alpha, v_ext, acc_ref):
    acc_ref[...] = _lane_tile(alpha, 2) * acc_ref[...] + _dot(p, v_ext)


def _flash_over_tiles(lead_rows, lead_mask, n_tiles, tile_rows, score_fn, value_fn, m_ref, acc_ref, scale=None):
    def two(rows_a, rows_b, mask_a=None):
        s_a, s_b = score_fn(rows_a), score_fn(rows_b)
        if mask_a is not None:
            s_a = jnp.where(mask_a, s_a, NEG_INF)
        _flash_update(s_a, value_fn(rows_a), m_ref, acc_ref, scale)
        _flash_update(s_b, value_fn(rows_b), m_ref, acc_ref, scale)

    @pl.when(n_tiles % 2 == 1)
    def _():
        two(lead_rows, tile_rows(n_tiles - 1), lead_mask)

    @pl.when(n_tiles % 2 == 0)
    def _():
        _flash_update(jnp.where(lead_mask, score_fn(lead_rows), NEG_INF), value_fn(lead_rows), m_ref, acc_ref,
                      scale)

    def pair(c, carry):
        two(tile_rows(2 * c), tile_rows(2 * c + 1))
        return carry

    lax.fori_loop(0, n_tiles // 2, pair, 0)


def _flash_reset(m_ref, acc_ref):
    m_ref[...] = jnp.full(m_ref.shape, NEG_INF, F32)
    acc_ref[...] = jnp.zeros(acc_ref.shape, F32)


def _flash_result(acc_ref):
    acc = acc_ref[...]
    return acc[:, :LANES] / acc[:, LANES:]


def _with_ones(v_bf16):
    return jnp.concatenate([v_bf16, jnp.ones(v_bf16.shape, BF16)], axis=1)


def _new_token_mask(rows, keys, t_pad, n_new):
    r = lax.broadcasted_iota(jnp.int32, (rows, keys), 0) % t_pad
    c = lax.broadcasted_iota(jnp.int32, (rows, keys), 1)
    return (c <= r) & (c < n_new)


def _head_rows(ref, idx, h, n_heads, page):
    return ref[idx + (pl.ds(h, page, stride=n_heads), slice(None))]


def _topk_mask(sc, lane, n_sel):
    sel = jnp.zeros(sc.shape, F32)
    for _ in range(n_sel):
        mx = jnp.max(sc, axis=-1, keepdims=True)
        idx = jnp.min(jnp.where(sc == mx, lane, LANES), axis=-1, keepdims=True)
        pick = lane == idx
        sel = jnp.where(pick & (mx > 0.5 * NEG_INF), 1.0, sel)
        sc = jnp.where(pick, BELOW_NEG_INF, sc)
    return sel


def _moba_prompt_kernel(q_ref, k_ref, v_ref, o_ref, kmean_ref, kext_ref, vext_ref, m_ref, acc_ref,
                        *, blk, span_blocks, scale, n_sel):
    i = pl.program_id(1)
    tq = q_ref.shape[0]
    s_len, dh = k_ref.shape
    nb = s_len // blk
    span = span_blocks * blk

    @pl.when(i == 0)
    def _():
        kmean_ref[...] = jnp.zeros_like(kmean_ref)
        kmean_ref[0:nb, :] = jnp.mean(k_ref[...].reshape(nb, blk, dh), axis=1)
        blk_lane = lax.broadcasted_iota(jnp.int32, (blk, LANES), 1)
        for n in range(nb):
            rows = slice(n * blk, (n + 1) * blk)
            kext_ref[rows, 0:LANES] = k_ref[rows, :].astype(BF16)
            kext_ref[rows, LANES:] = jnp.where(blk_lane == n, 1.0, 0.0).astype(BF16)
            vext_ref[rows, :] = _with_ones(v_ref[rows, :].astype(BF16))

    q = q_ref[...]
    lane = lax.broadcasted_iota(jnp.int32, (tq, LANES), 1)
    own = (i * tq + lax.broadcasted_iota(jnp.int32, (tq, LANES), 0)) // blk
    sc = _dot_nt(q, kmean_ref[...], precision=lax.Precision.HIGHEST)
    sel = _topk_mask(jnp.where(lane < own, sc, NEG_INF), lane, n_sel)
    sel = jnp.where(lane == own, 1.0, sel)
    q_ext = jnp.concatenate([q.astype(BF16), ((1.0 - sel) * NEG_INF).astype(BF16)], axis=1)

    _flash_reset(m_ref, acc_ref)
    last = (i * tq) // span
    q_pos = i * tq + lax.broadcasted_iota(jnp.int32, (tq, span), 0)
    k_pos = last * span + lax.broadcasted_iota(jnp.int32, (tq, span), 1)

    def span_rows(c):
        return pl.ds(pl.multiple_of(c * span, span), span)

    _flash_over_tiles(span_rows(last), k_pos <= q_pos, last, span_rows,
                      lambda rows: _dot_nt(q_ext, kext_ref[rows, :]), lambda rows: vext_ref[rows, :],
                      m_ref, acc_ref, scale)
    o_ref[...] = _flash_result(acc_ref).astype(BF16)


def _moba_prompt(h_all, cols, n_heads, dh):
    s = h_all.shape[0]
    blk = MOBA_BLOCK
    nb = s // blk
    span_blocks = math.gcd(nb, 4)
    tq = math.gcd(span_blocks, 2) * blk
    assert s % blk == 0 and nb <= LANES and dh == LANES
    cq, ck, cv = cols["mq"] // dh, cols["mk"] // dh, cols["mv"] // dh
    return pl.pallas_call(
        functools.partial(_moba_prompt_kernel, blk=blk, span_blocks=span_blocks, scale=dh ** -0.5,
                          n_sel=min(MOBA_TOPK, nb - 1)),
        out_shape=jax.ShapeDtypeStruct((s, n_heads * dh), BF16),
        grid=(n_heads, s // tq),
        in_specs=[pl.BlockSpec((tq, dh), lambda h, i: (i, cq + h)),
                  pl.BlockSpec((s, dh), lambda h, i: (0, ck + h)),
                  pl.BlockSpec((s, dh), lambda h, i: (0, cv + h))],
        out_specs=pl.BlockSpec((tq, dh), lambda h, i: (i, h)),
        scratch_shapes=[pltpu.VMEM((LANES, dh), F32), pltpu.VMEM((s, 2 * LANES), BF16),
                        pltpu.VMEM((s, 2 * LANES), BF16), pltpu.VMEM((tq, LANES), F32),
                        pltpu.VMEM((tq, 2 * LANES), F32)],
        compiler_params=_params("arbitrary", "arbitrary"),
        name="moba_prompt",
    )(h_all, h_all, h_all)


def _moba_sample_kernel(pt_ref, q_ref, *refs, pps, ppb, n_heads, t_pad, n_new, scale, n_sel):
    k_refs, v_refs = refs[:pps], refs[pps:2 * pps]
    kn_ref, vn_ref, o_ref, m_ref, l_ref, sc_ref, part_ref = refs[2 * pps:]
    step = pl.program_id(1)
    page = kn_ref.shape[1] // n_heads
    lane = lax.broadcasted_iota(jnp.int32, (t_pad, LANES), 1)

    @pl.when(step == 0)
    def _():
        m_ref[...] = jnp.full(m_ref.shape, NEG_INF, F32)
        l_ref[...] = jnp.zeros(l_ref.shape, F32)
        sc_ref[...] = jnp.full(sc_ref.shape, NEG_INF, F32)

    blocks = pps // ppb
    blk = ppb * page
    s_raw = [_dot_nt(q_ref[0, h], jnp.concatenate(
        [_head_rows(r, (0, 0), h, n_heads, page) for r in k_refs], 0).astype(BF16)) for h in range(n_heads)]
    m_new, l_new, sc_new = ([m_ref[h] for h in range(n_heads)], [l_ref[h] for h in range(n_heads)],
                            [sc_ref[h] for h in range(n_heads)])
    probs = []
    for h in range(n_heads):
        for j in range(blocks):
            s_blk = s_raw[h][:, j * blk:(j + 1) * blk]
            m_n = jnp.max(s_blk, axis=-1, keepdims=True) * scale
            probs.append(jnp.exp(s_blk * scale - m_n).astype(BF16))
            here = lane == step * blocks + j
            m_new[h] = jnp.where(here, m_n, m_new[h])
            sc_new[h] = jnp.where(here, jnp.mean(s_blk, axis=-1, keepdims=True), sc_new[h])
    parts = []
    for h in range(n_heads):
        for j in range(blocks):
            vb = jnp.concatenate([_head_rows(v_refs[g], (0, 0), h, n_heads, page)
                                  for g in range(j * ppb, (j + 1) * ppb)], 0).astype(BF16)
            part = _dot(probs[h * blocks + j], _with_ones(vb))
            parts.append(part[:, :LANES])
            l_new[h] = jnp.where(lane == step * blocks + j, part[:, LANES:], l_new[h])
    part_ref[step] = jnp.concatenate(parts, axis=0)
    for h in range(n_heads):
        m_ref[h], l_ref[h], sc_ref[h] = m_new[h], l_new[h], sc_new[h]

    @pl.when(step == pl.num_programs(1) - 1)
    def _():
        own_mask = _new_token_mask(t_pad, page, t_pad, n_new)
        s_own = [_dot_nt(q_ref[0, h], _head_rows(kn_ref, (0,), h, n_heads, page).astype(BF16))
                 for h in range(n_heads)]
        p_own, w_blocks, l_all = [], [], []
        for h in range(n_heads):
            sel = _topk_mask(sc_ref[h], lane, n_sel) > 0.5
            s = jnp.where(own_mask, s_own[h] * scale, NEG_INF)
            m_blocks = m_ref[h]
            m_all = jnp.maximum(jnp.max(jnp.where(sel, m_blocks, NEG_INF), axis=-1, keepdims=True),
                                jnp.max(s, axis=-1, keepdims=True))
            p = jnp.exp(s - m_all)
            w = jnp.where(sel, jnp.exp(m_blocks - m_all), 0.0)
            l_all.append(jnp.sum(w * l_ref[h], axis=-1, keepdims=True) + jnp.sum(p, axis=-1, keepdims=True))
            p_own.append(p.astype(BF16))
            w_blocks.append(w)
        for h in range(n_heads):
            acc = _dot(p_own[h], _head_rows(vn_ref, (0,), h, n_heads, page).astype(BF16))
            for b in range(part_ref.shape[0] * blocks):
                row = (h * blocks + b % blocks) * t_pad
                acc = acc + w_blocks[h][:, b:b + 1] * part_ref[b // blocks, row:row + t_pad, :]
            o_ref[0, :, h * LANES:(h + 1) * LANES] = (acc / l_all[h]).astype(BF16)


def _moba_sample(layer, q, k_new, v_new, cache_k, cache_v, page_table, n_heads, n_new, pps):
    b, _, t_pad, dh = q.shape
    page = cache_k.shape[2] // n_heads
    n_pages = page_table.shape[1]
    ppb = MOBA_BLOCK // page
    nb = n_pages // ppb
    assert ppb * page == MOBA_BLOCK and n_pages % pps == 0 and pps % ppb == 0 and nb <= LANES and dh == LANES

    def page_spec(g):
        return pl.BlockSpec((1, 1, page * n_heads, dh), lambda i, n, pt: (layer, pt[i, n * pps + g], 0, 0))

    new_spec = pl.BlockSpec((1, page * n_heads, dh), lambda i, n, pt: (i, 0, 0))
    stat = pltpu.VMEM((n_heads, t_pad, LANES), F32)
    grid_spec = pltpu.PrefetchScalarGridSpec(
        num_scalar_prefetch=1,
        grid=(b, n_pages // pps),
        in_specs=[pl.BlockSpec((1, n_heads, t_pad, dh), lambda i, n, pt: (i, 0, 0, 0))]
                 + [page_spec(g) for g in range(pps)] * 2 + [new_spec, new_spec],
        out_specs=pl.BlockSpec((1, t_pad, n_heads * dh), lambda i, n, pt: (i, 0, 0)),
        scratch_shapes=[stat, stat, stat,
                        pltpu.VMEM((n_pages // pps, (pps // ppb) * n_heads * t_pad, dh), F32)],
    )
    return pl.pallas_call(
        functools.partial(_moba_sample_kernel, pps=pps, ppb=ppb, n_heads=n_heads, t_pad=t_pad, n_new=n_new,
                          scale=dh ** -0.5, n_sel=min(MOBA_TOPK, nb)),
        out_shape=jax.ShapeDtypeStruct((b, t_pad, n_heads * dh), BF16),
        grid_spec=grid_spec,
        compiler_params=_params("parallel", "arbitrary"),
        name="moba_sample",
    )(page_table, q, *([cache_k] * pps), *([cache_v] * pps), k_new, v_new)


def _diff_lambda(lqk, lam_init):
    s1 = jnp.sum(lqk[0:1] * lqk[1:2], axis=-1, keepdims=True)
    s2 = jnp.sum(lqk[2:3] * lqk[3:4], axis=-1, keepdims=True)
    return jnp.exp(s1) - jnp.exp(s2) + lam_init


def _diff_finish(o1, o2, lqk, g, lam_init):
    od = o1 - _diff_lambda(lqk, lam_init) * o2
    od = od * lax.rsqrt(jnp.mean(od * od, axis=-1, keepdims=True) + NORM_EPS) * g
    return od * (1.0 - lam_init)


def _split_sub_heads(q):
    first = lax.broadcasted_iota(jnp.int32, q.shape, q.ndim - 1) < LANES // 2
    return jnp.concatenate([jnp.where(first, q, 0.0), jnp.where(first, 0.0, q)], axis=q.ndim - 2)


def _diff_prompt_kernel(q_ref, k_ref, v_ref, g_ref, lqk_ref, o_ref, kb_ref, vext_ref, m_ref, acc_ref,
                        *, tile, scale, lam_init):
    i = pl.program_id(1)
    s_len = k_ref.shape[0]

    @pl.when(i == 0)
    def _():
        for c in range(s_len // tile):
            rows = slice(c * tile, (c + 1) * tile)
            kb_ref[rows, :] = k_ref[rows, :].astype(BF16)
            vext_ref[rows, :] = _with_ones(v_ref[rows, :].astype(BF16))

    fold = _is_pow2(scale)
    qs = _split_sub_heads(q_ref[...])
    qs = (qs * scale if fold else qs).astype(BF16)

    def scores(rows):
        s = _dot_nt(qs, kb_ref[rows, :])
        return s if fold else s * scale

    _flash_reset(m_ref, acc_ref)
    row = lax.broadcasted_iota(jnp.int32, (2 * tile, tile), 0) % tile
    col = lax.broadcasted_iota(jnp.int32, (2 * tile, tile), 1)

    def tile_rows(n):
        return pl.ds(pl.multiple_of(n * tile, tile), tile)

    _flash_over_tiles(tile_rows(i), col <= row, i, tile_rows, scores, lambda rows: vext_ref[rows, :],
                      m_ref, acc_ref)
    o = _flash_result(acc_ref)
    o_ref[...] = _diff_finish(o[:tile], o[tile:], lqk_ref[...], g_ref[...], lam_init).astype(BF16)


def _diff_prompt(h_all, cols, lqk, subln_g, lam_init, n_heads, dh, tile):
    s = h_all.shape[0]
    w = 2 * dh
    assert w == LANES and s % tile == 0
    cq, ck, cv = cols["dq"] // w, cols["dk"] // w, cols["dv"] // w
    return pl.pallas_call(
        functools.partial(_diff_prompt_kernel, tile=tile, scale=dh ** -0.5, lam_init=lam_init),
        out_shape=jax.ShapeDtypeStruct((s, n_heads * w), BF16),
        grid=(n_heads, s // tile),
        in_specs=[pl.BlockSpec((tile, w), lambda h, i: (i, cq + h)),
                  pl.BlockSpec((s, w), lambda h, i: (0, ck + h)),
                  pl.BlockSpec((s, w), lambda h, i: (0, cv + h)),
                  pl.BlockSpec((1, w), lambda h, i: (0, 0)),
                  pl.BlockSpec(lqk.shape, lambda h, i: (0, 0))],
        out_specs=pl.BlockSpec((tile, w), lambda h, i: (i, h)),
        scratch_shapes=[pltpu.VMEM((s, w), BF16), pltpu.VMEM((s, 2 * LANES), BF16),
                        pltpu.VMEM((2 * tile, LANES), F32), pltpu.VMEM((2 * tile, 2 * LANES), F32)],
        compiler_params=_params("arbitrary", "arbitrary"),
        name="diff_prompt",
    )(h_all, h_all, h_all, subln_g.reshape(1, w), lqk)


def _diff_sample_kernel(pt_ref, q_ref, *refs, pps, n_heads, t_pad, n_new, lam_init):
    k_refs, v_refs = refs[:pps], refs[pps:2 * pps]
    kn_ref, vn_ref, g_ref, lqk_ref, o_ref, m_ref, acc_ref = refs[2 * pps:]
    step = pl.program_id(1)
    page = kn_ref.shape[2]

    def head_dims(h):
        return slice(h * LANES, (h + 1) * LANES)

    @pl.when(step == 0)
    def _():
        _flash_reset(m_ref, acc_ref)
        new_mask = _new_token_mask(2 * t_pad, page, t_pad, n_new)
        new_s = [_dot(q_ref[0, h], kn_ref[0, head_dims(h), :].astype(BF16)) for h in range(n_heads)]
        new_w = [_flash_weights(jnp.where(new_mask, new_s[h], NEG_INF), m_ref.at[h]) for h in range(n_heads)]
        for h in range(n_heads):
            _flash_accumulate(*new_w[h], _with_ones(_head_rows(vn_ref, (0,), h, n_heads, page).astype(BF16)),
                              acc_ref.at[h])

    scores = [_dot(q_ref[0, h], jnp.concatenate([r[0, 0, head_dims(h), :] for r in k_refs], 1).astype(BF16))
              for h in range(n_heads)]
    weights = [_flash_weights(scores[h], m_ref.at[h]) for h in range(n_heads)]
    for h in range(n_heads):
        vb = jnp.concatenate([_head_rows(r, (0, 0), h, n_heads, page) for r in v_refs], 0).astype(BF16)
        _flash_accumulate(*weights[h], _with_ones(vb), acc_ref.at[h])

    @pl.when(step == pl.num_programs(1) - 1)
    def _():
        for h in range(n_heads):
            o = _flash_result(acc_ref.at[h])
            o_ref[0, :, h * LANES:(h + 1) * LANES] = _diff_finish(
                o[:t_pad], o[t_pad:], lqk_ref[...], g_ref[...], lam_init).astype(BF16)


def _diff_sample(layer, q, k_new, v_new, cache_k, cache_v, page_table, lqk, subln_g, lam_init,
                 n_heads, n_new, pps):
    b, _, rows, w = q.shape
    t_pad = rows // 2
    page = cache_k.shape[3]
    n_pages = page_table.shape[1]
    assert n_pages % pps == 0 and w == LANES and page == LANES

    def page_spec(g, shape):
        return pl.BlockSpec((1, 1) + shape, lambda i, n, pt: (layer, pt[i, n * pps + g], 0, 0))

    def new_spec(shape):
        return pl.BlockSpec((1,) + shape, lambda i, n, pt: (i, 0, 0))

    k_shape, v_shape = (n_heads * w, page), (page * n_heads, w)
    grid_spec = pltpu.PrefetchScalarGridSpec(
        num_scalar_prefetch=1,
        grid=(b, n_pages // pps),
        in_specs=[pl.BlockSpec((1, n_heads, rows, w), lambda i, n, pt: (i, 0, 0, 0))]
                 + [page_spec(g, k_shape) for g in range(pps)] + [page_spec(g, v_shape) for g in range(pps)]
                 + [new_spec(k_shape), new_spec(v_shape),
                    pl.BlockSpec((1, LANES), lambda i, n, pt: (0, 0)),
                    pl.BlockSpec(lqk.shape, lambda i, n, pt: (0, 0))],
        out_specs=pl.BlockSpec((1, t_pad, n_heads * w), lambda i, n, pt: (i, 0, 0)),
        scratch_shapes=[pltpu.VMEM((n_heads, rows, LANES), F32), pltpu.VMEM((n_heads, rows, 2 * LANES), F32)],
    )
    return pl.pallas_call(
        functools.partial(_diff_sample_kernel, pps=pps, n_heads=n_heads, t_pad=t_pad, n_new=n_new,
                          lam_init=lam_init),
        out_shape=jax.ShapeDtypeStruct((b, t_pad, n_heads * w), BF16),
        grid_spec=grid_spec,
        compiler_params=_params("parallel", "arbitrary"),
        name="diff_sample",
    )(page_table, q, *([cache_k] * pps), *([cache_v] * pps), k_new, v_new, subln_g.reshape(1, LANES), lqk)


def _merge_kernel(x_ref, yr_ref, om_ref, od_ref, gates_ref, wr_ref, wm_ref, wd_ref, wo_ref, o_ref):
    d = x_ref.shape[1]
    mix = (jax.nn.sigmoid(gates_ref[:, 0:d]) * _dot(yr_ref[...], wr_ref[...])
           + jax.nn.sigmoid(gates_ref[:, d:2 * d]) * _dot(om_ref[...], wm_ref[...])
           + jax.nn.sigmoid(gates_ref[:, 2 * d:3 * d]) * _dot(od_ref[...], wd_ref[...]))
    o_ref[...] = x_ref[...] + _dot(mix.astype(BF16), wo_ref[...])


def _merge(x, y_ret, o_moba, o_diff, h_all, gate_col, w_ret, w_moba, w_diff, w_out, layer, tm):
    m, d = x.shape
    assert gate_col % (3 * d) == 0

    def rows(w):
        return pl.BlockSpec((tm, w), lambda i: (i, 0))

    def whole(arr):
        return pl.BlockSpec((None,) + arr.shape[1:], lambda i: (layer, 0, 0), pipeline_mode=pl.Buffered(1))

    return pl.pallas_call(
        _merge_kernel,
        out_shape=jax.ShapeDtypeStruct((m, d), F32),
        grid=(m // tm,),
        in_specs=[rows(d), rows(y_ret.shape[1]), rows(o_moba.shape[1]), rows(o_diff.shape[1]),
                  pl.BlockSpec((tm, 3 * d), lambda i: (i, gate_col // (3 * d))),
                  whole(w_ret), whole(w_moba), whole(w_diff), whole(w_out)],
        out_specs=rows(d),
        compiler_params=_params("parallel"),
        name="merge_out",
    )(x, y_ret, o_moba, o_diff, h_all, w_ret, w_moba, w_diff, w_out)


def _ffn_kernel(x_ref, g_ref, w1_ref, w2_ref, fg_ref, o_ref, xn_ref, *, final_norm):
    t = pl.program_id(1)

    @pl.when(t == 0)
    def _():
        x = x_ref[...]
        xn_ref[...] = _rms(x, g_ref[...]).astype(BF16)
        o_ref[...] = x

    hid = jnp.square(jnp.maximum(_dot(xn_ref[...], w1_ref[...]), 0.0))
    o_ref[...] += _dot(hid.astype(BF16), w2_ref[...])

    if final_norm:
        @pl.when(t == pl.num_programs(1) - 1)
        def _():
            o_ref[...] = _rms(o_ref[...], fg_ref[...])


def _ffn(x, g, w1, w2, layer, final_g, final_norm, tm, tf):
    m, d = x.shape
    f = w1.shape[2]
    return pl.pallas_call(
        functools.partial(_ffn_kernel, final_norm=final_norm),
        out_shape=jax.ShapeDtypeStruct((m, d), F32),
        grid=(m // tm, f // tf),
        in_specs=[pl.BlockSpec((tm, d), lambda i, t: (i, 0)),
                  pl.BlockSpec((1, d), lambda i, t: (0, 0)),
                  pl.BlockSpec((None, d, tf), lambda i, t: (layer, 0, t)),
                  pl.BlockSpec((None, tf, d), lambda i, t: (layer, t, 0)),
                  pl.BlockSpec((1, d), lambda i, t: (0, 0))],
        out_specs=pl.BlockSpec((tm, d), lambda i, t: (i, 0)),
        scratch_shapes=[pltpu.VMEM((tm, d), BF16)],
        compiler_params=_params("parallel", "arbitrary"),
        name="ffn",
    )(x, g.reshape(1, d), w1, w2, final_g.reshape(1, d))


def _pad_rows(x, rows):
    return jnp.pad(x, ((0, 0), (0, rows - x.shape[1]), (0, 0)))


def _per_head(x, n_heads, t_pad):
    b = x.shape[0]
    return _pad_rows(x, t_pad).reshape(b, t_pad, n_heads, LANES).transpose(0, 2, 1, 3)


def _as_page(x, n_heads, page):
    return _pad_rows(x, page).reshape(x.shape[0], page * n_heads, LANES)


def kernel(x_prompt, x_sample, cache_moba_k, cache_moba_v, cache_diff_k, cache_diff_v, state_ret, page_table, norm1_g, w_in, diff_lq1, diff_lk1, diff_lq2, diff_lk2, diff_subln_g, w_up_ret, w_up_moba, w_up_diff, w_out, norm2_g, w_ff1, w_ff2, final_g):
    bp, s_len, d = x_prompt.shape
    bs, t_new, _ = x_sample.shape
    depth, n_pool, page, moba_h, moba_dh = cache_moba_k.shape
    diff_h, diff_dh = cache_diff_k.shape[3], cache_diff_k.shape[5]
    ret_h, ret_dk, ret_dv = state_ret.shape[2:]
    n_pages = page_table.shape[1]
    past_len = n_pages * page
    assert bp == 1 and ret_dk == LANES and moba_dh == LANES and 2 * diff_dh == LANES
    assert t_new <= SUBLANES and past_len % MOBA_BLOCK == 0

    ret_qk_w, ret_v_w = ret_h * ret_dk, ret_h * ret_dv
    moba_w, diff_w = moba_h * moba_dh, diff_h * 2 * diff_dh
    names = ["rq", "rk", "rv", "rg", "mq", "mk", "mv", "dq", "dk", "dv", "gr", "gm", "gd"]
    widths = [ret_qk_w, ret_qk_w, ret_v_w, ret_v_w, moba_w, moba_w, moba_w, diff_w, diff_w, diff_w, d, d, d]
    cols, off = {}, 0
    for nm, wd in zip(names, widths):
        cols[nm] = off
        off += wd
    in_w = off
    assert w_in.shape == (depth, d, in_w)

    tn = 512
    kinds = {"rq": EPI_ROPE128, "rk": EPI_ROPE128_SCALED, "mq": EPI_ROPE128, "mk": EPI_ROPE128,
             "dq": EPI_ROPE64, "dk": EPI_ROPE64}
    tile_kinds = []
    for nm, wd in zip(names, widths):
        assert wd % tn == 0
        tile_kinds += [kinds.get(nm, EPI_NONE)] * (wd // tn)
    tile_kinds = tuple(tile_kinds)
    assert moba_w == tn and diff_w == tn and moba_h == diff_h
    copy_tiles = tuple(cols[nm] // tn for nm in ("mk", "mv", "dv"))

    m_s = bs * t_new
    xp = x_prompt.reshape(s_len, d)
    xs = x_sample.reshape(m_s, d)
    tables_p = _rope_tables(np.arange(s_len))
    tables_s = _rope_tables(np.tile(past_len + np.arange(t_new), bs))
    tm_p = min(1024, s_len)
    tm_ffn = min(512, s_len)
    tm_merge = min(256, s_len)
    tf = min(1024, w_ff1.shape[-1])
    diff_tile = min(512, s_len)
    ret_chunk = min(256, s_len)
    t_pad = SUBLANES
    pps = min(8, n_pages)

    ck_moba = cache_moba_k.reshape(depth, n_pool, page * moba_h, moba_dh)
    cv_moba = cache_moba_v.reshape(depth, n_pool, page * moba_h, moba_dh)
    ck_diff = cache_diff_k.transpose(0, 1, 3, 4, 5, 2).reshape(depth, n_pool, diff_w, page)
    cv_diff = cache_diff_v.reshape(depth, n_pool, page * diff_h, 2 * diff_dh)

    w_in_b, w_ret_b, w_moba_b, w_diff_b, w_out_b, w_ff1_b, w_ff2_b = (
        w.astype(BF16) for w in (w_in, w_up_ret, w_up_moba, w_up_diff, w_out, w_ff1, w_ff2))

    outs = {k: [] for k in ["pmk", "pmv", "pdk", "pdv", "pst", "smk", "smv", "sdk", "sdv", "sst"]}
    for l in range(depth):
        lam_init = 0.8 - 0.6 * math.exp(-0.3 * l)
        lqk = jnp.stack([diff_lq1[l], diff_lk1[l], diff_lq2[l], diff_lk2[l]])
        last = l == depth - 1
        k_scale = ret_dk ** -0.5

        def sec(h_all, nm, wd):
            return h_all[:, cols[nm]:cols[nm] + wd]

        hp, pmk, pmv, pdv = _in_proj(xp, norm1_g[l], w_in_b, l, tables_p, tile_kinds, copy_tiles, k_scale,
                                     tm_p, tn)
        y_ret, s_fin = _ret_prompt(hp, cols, ret_h, ret_dk, ret_dv, ret_chunk)
        o_moba = _moba_prompt(hp, cols, moba_h, moba_dh)
        o_diff = _diff_prompt(hp, cols, lqk, diff_subln_g[l], lam_init, diff_h, diff_dh, diff_tile)
        x1 = _merge(xp, y_ret, o_moba, o_diff, hp, cols["gr"], w_ret_b, w_moba_b, w_diff_b, w_out_b, l, tm_merge)
        xp = _ffn(x1, norm2_g[l], w_ff1_b, w_ff2_b, l, final_g, last, tm_ffn, tf)
        outs["pmk"].append(pmk.reshape(bp, s_len, moba_h, moba_dh))
        outs["pmv"].append(pmv.reshape(bp, s_len, moba_h, moba_dh))
        outs["pdk"].append(sec(hp, "dk", diff_w).reshape(bp, s_len, diff_h, 2, diff_dh))
        outs["pdv"].append(pdv.reshape(bp, s_len, diff_h, 2 * diff_dh))
        outs["pst"].append(s_fin.reshape(bp, ret_h, ret_dk, ret_dv))

        hs = _in_proj(xs, norm1_g[l], w_in_b, l, tables_s, tile_kinds, (), k_scale, m_s, tn)[0]

        def sec3(nm, wd):
            return sec(hs, nm, wd).reshape(bs, t_new, wd)

        y_ret_s, s_new = _ret_sample(
            _pad_rows(sec3("rq", ret_qk_w), t_pad), _pad_rows(sec3("rk", ret_qk_w), t_pad),
            _pad_rows(sec3("rv", ret_v_w), t_pad), _pad_rows(sec3("rg", ret_v_w), t_pad),
            state_ret, l, ret_h, ret_dk, ret_dv, t_new)
        mk_s, mv_s = sec3("mk", moba_w), sec3("mv", moba_w)
        o_moba_s = _moba_sample(l, _per_head(sec3("mq", moba_w), moba_h, t_pad).astype(BF16),
                                _as_page(mk_s, moba_h, page), _as_page(mv_s, moba_h, page),
                                ck_moba, cv_moba, page_table, moba_h, t_new, pps)
        dk_s, dv_s = sec3("dk", diff_w), sec3("dv", diff_w)
        diff_scale = diff_dh ** -0.5
        assert _is_pow2(diff_scale)
        dq_s = _split_sub_heads(_per_head(sec3("dq", diff_w) * diff_scale, diff_h, t_pad)).astype(BF16)
        o_diff_s = _diff_sample(l, dq_s, _pad_rows(dk_s, page).transpose(0, 2, 1), _as_page(dv_s, diff_h, page),
                                ck_diff, cv_diff, page_table, lqk, diff_subln_g[l], lam_init,
                                diff_h, t_new, pps)

        def unpad(o):
            return o[:, :t_new].reshape(m_s, o.shape[-1])

        x1s = _merge(xs, unpad(y_ret_s), unpad(o_moba_s), unpad(o_diff_s), hs, cols["gr"],
                     w_ret_b, w_moba_b, w_diff_b, w_out_b, l, m_s)
        xs = _ffn(x1s, norm2_g[l], w_ff1_b, w_ff2_b, l, final_g, last, m_s, tf)
        outs["smk"].append(mk_s.reshape(bs, t_new, moba_h, moba_dh))
        outs["smv"].append(mv_s.reshape(bs, t_new, moba_h, moba_dh))
        outs["sdk"].append(dk_s.reshape(bs, t_new, diff_h, 2, diff_dh))
        outs["sdv"].append(dv_s.reshape(bs, t_new, diff_h, 2 * diff_dh))
        outs["sst"].append(s_new)

    st = {k: jnp.stack(v) for k, v in outs.items()}
    return (xp.reshape(bp, s_len, d), xs.reshape(bs, t_new, d),
            st["pmk"], st["pmv"], st["pdk"], st["pdv"], st["pst"],
            st["smk"], st["smv"], st["sdk"], st["sdv"], st["sst"])
```
